```python
import math
import jax
import jax.numpy as jnp
from jax import lax
import numpy as np

D_MODEL = 1024
BATCH = 8
SEQ = 2048
DEPTH = 4

GRID_W = 64
CTX_LEN = 256
CHUNK = 64
NORM_EPS = 1e-6
F_TINY = 1e-30

DN_HEADS = 4
DN_DK = 128
DN_DV = 128
DN_QK = DN_HEADS * DN_DK
DN_V = DN_HEADS * DN_DV
CONV_K = 3

HG_HEADS = 4
HG_DK = 128
HG_DV = 128
HG_K = HG_HEADS * HG_DK
HG_V = HG_HEADS * HG_DV

N_GROUPS = 4
EXPERTS_PER_GROUP = 8
N_EXPERTS = N_GROUPS * EXPERTS_PER_GROUP
TOP_K_IN_GROUP = 2
D_EXPERT = D_MODEL // 2
MOE_BLOCK = 128

IN_SIZES = (2 * DN_QK + DN_V, DN_V, 2 * DN_HEADS, 2 * DN_HEADS, HG_K, 2 * HG_K, HG_V, HG_V, 2 * D_MODEL)
IN_SPLITS = tuple(sum(IN_SIZES[:i + 1]) for i in range(len(IN_SIZES) - 1))
N_IN = sum(IN_SIZES)

kernel_name = 'hybrid_deltanet_hgrn2_hmoe_dit'

F32 = jnp.float32


def _rmsnorm(x, g):
    xf = x.astype(F32)
    y = xf * lax.rsqrt(jnp.mean(xf * xf, axis=-1, keepdims=True) + NORM_EPS)
    return (y * g.astype(F32)).astype(x.dtype)


def _l2norm(x):
    xf = x.astype(F32)
    return xf * lax.rsqrt(jnp.sum(xf * xf, axis=-1, keepdims=True) + NORM_EPS)


def _masked_exp(diff, mask):
    return jnp.where(mask, jnp.exp(jnp.where(mask, diff, 0.0)), 0.0)


def _gated_rmsnorm(o, z, g):
    bsz, n, h, dv = o.shape
    y = o * lax.rsqrt(jnp.mean(o * o, axis=-1, keepdims=True) + NORM_EPS) * g.astype(F32)
    y = y * jax.nn.silu(z.astype(F32).reshape(bsz, n, h, dv))
    return y.reshape(bsz, n, h * dv).astype(z.dtype)


def _short_conv(x, w):
    pad = CONV_K // 2
    return lax.conv_general_dilated(x, w[:, None, :].astype(x.dtype), window_strides=(1,),
                                    padding=[(pad, pad)], dimension_numbers=('NWC', 'WIO', 'NWC'),
                                    feature_group_count=x.shape[-1])


def _to_col_major(t):
    bsz, n, f = t.shape
    rows = n // GRID_W
    return t.reshape(bsz, rows, GRID_W, f).transpose(0, 2, 1, 3).reshape(bsz, n, f)


def _from_col_major(t):
    bsz, n, f = t.shape
    rows = n // GRID_W
    return t.reshape(bsz, GRID_W, rows, f).transpose(0, 2, 1, 3).reshape(bsz, n, f)


def _rev(t, direction):
    return jnp.flip(t, axis=1) if direction == 1 else t


def _to_chunks(t):
    bsz, n, h = t.shape[:3]
    t = t.reshape(bsz, n // CHUNK, CHUNK, h, *t.shape[3:])
    return jnp.moveaxis(t, (1, 3), (0, 2))


def _from_chunks(t):
    t = jnp.moveaxis(t, (0, 2), (1, 3))
    return t.reshape(t.shape[0], t.shape[1] * t.shape[2], *t.shape[3:])


def gated_delta_chunked(q, k, v, g, beta, s0, return_out):
    q, k, v, g, beta = (_to_chunks(t.astype(F32)) for t in (q, k, v, g, beta))
    dv = v.shape[-1]
    incl = jnp.tril(jnp.ones((CHUNK, CHUNK), bool))
    strict = jnp.tril(jnp.ones((CHUNK, CHUNK), bool), -1)
    gcum = jnp.cumsum(g, axis=-1)
    decay = _masked_exp(gcum[..., :, None] - gcum[..., None, :], incl)
    kb = k * beta[..., None]
    a = jnp.where(strict, jnp.einsum('nbhik,nbhjk->nbhij', kb, k) * decay, 0.0)
    rhs = jnp.concatenate([v * beta[..., None], kb * jnp.exp(gcum)[..., None]], axis=-1)
    sol = lax.linalg.triangular_solve(a, rhs, left_side=True, lower=True, unit_diagonal=True)
    u, w = sol[..., :dv], sol[..., dv:]

    def step(s, inp):
        qc, kc, uc, wc, gc, dc = inp
        v_new = uc - jnp.einsum('bhck,bhkv->bhcv', wc, s)
        g_last = gc[..., -1]
        s_new = s * jnp.exp(g_last)[..., None, None] + jnp.einsum(
            'bhck,bhcv->bhkv', kc * jnp.exp(g_last[..., None] - gc)[..., None], v_new)
        if not return_out:
            return s_new, None
        o = jnp.einsum('bhck,bhkv->bhcv', qc * jnp.exp(gc)[..., None], s) + jnp.einsum(
            'bhij,bhjv->bhiv', jnp.einsum('bhik,bhjk->bhij', qc, kc) * dc, v_new)
        return s_new, o

    s_fin, o = lax.scan(step, s0.astype(F32), (q, k, u, w, gcum, decay))
    return (_from_chunks(o) if return_out else None), s_fin


def gla_chunked(q, k, v, log_f, s0, return_out):
    q, k, v, log_f = (_to_chunks(t.astype(F32)) for t in (q, k, v, log_f))
    gcum = jnp.cumsum(log_f, axis=3)
    incl = jnp.tril(jnp.ones((CHUNK, CHUNK), bool))[..., None]

    def step(s, inp):
        qc, kc, vc, gc = inp
        g_last = gc[:, :, -1]
        s_new = s * jnp.exp(g_last)[..., None] + jnp.einsum(
            'bhck,bhcv->bhkv', kc * jnp.exp(g_last[:, :, None] - gc), vc)
        if not return_out:
            return s_new, None
        rel = _masked_exp(gc[:, :, :, None] - gc[:, :, None], incl)
        att = jnp.einsum('bhik,bhjk,bhijk->bhij', qc, kc, rel)
        o = jnp.einsum('bhck,bhkv->bhcv', qc * jnp.exp(gc), s) + jnp.einsum('bhij,bhjv->bhiv', att, vc)
        return s_new, o

    s_fin, o = lax.scan(step, s0.astype(F32), (q, k, v, gcum))
    return (_from_chunks(o) if return_out else None), s_fin


def _dn_inputs(qkv, b, a, conv_w, a_log, dt_bias):
    bsz, n = qkv.shape[:2]
    qkv = jax.nn.silu(_short_conv(qkv, conv_w))
    q, k, v = jnp.split(qkv, [DN_QK, 2 * DN_QK], axis=-1)
    q = _l2norm(q.reshape(bsz, n, DN_HEADS, DN_DK)) * (DN_DK ** -0.5)
    k = _l2norm(k.reshape(bsz, n, DN_HEADS, DN_DK))
    v = v.reshape(bsz, n, DN_HEADS, DN_DV)
    beta = jax.nn.sigmoid(b.astype(F32).reshape(bsz, n, 2, DN_HEADS))
    g = -jnp.exp(a_log.astype(F32)) * jax.nn.softplus(
        a.astype(F32).reshape(bsz, n, 2, DN_HEADS) + dt_bias.astype(F32))
    return q, k, v, beta, g


def _deltanet_scans(lat, ctx, ctx_out):
    xq, xk, xv, xb, xg = lat
    cq, ck, cv, cb, cg = ctx
    o_x, o_c = 0.0, 0.0
    for d in range(2):
        s0 = jnp.zeros((xq.shape[0], DN_HEADS, DN_DK, DN_DV), F32)
        oc, s_ctx = gated_delta_chunked(*(_rev(t, d) for t in (cq, ck, cv, cg[:, :, d], cb[:, :, d])), s0, ctx_out)
        ox, _ = gated_delta_chunked(*(_rev(t, d) for t in (xq, xk, xv, xg[:, :, d], xb[:, :, d])), s_ctx, True)
        o_x = o_x + _rev(ox, d)
        if ctx_out:
            o_c = o_c + _rev(oc, d)
    return o_x, (o_c if ctx_out else None)


def _hg_inputs(q, f2, i, lb):
    bsz, n = q.shape[:2]
    q = jax.nn.silu(q.astype(F32)).reshape(bsz, n, HG_HEADS, HG_DK)
    z = f2.astype(F32).reshape(bsz, n, 2, HG_HEADS, HG_DK)
    f = lb + (1.0 - lb) * jax.nn.sigmoid(z)
    log_f = jnp.log(jnp.maximum(f, F_TINY))
    k = (1.0 - lb) * jax.nn.sigmoid(-z)
    v = i.astype(F32).reshape(bsz, n, HG_HEADS, HG_DV)
    return q, k, log_f, v


def _hgrn2_scans(lat, ctx, ctx_out):
    xq, xk, xlf, xv = lat
    cq, ck, clf, cv = ctx
    o_x, o_c = 0.0, 0.0
    for d in range(2):
        s0 = jnp.zeros((xq.shape[0], HG_HEADS, HG_DK, HG_DV), F32)
        oc, s_ctx = gla_chunked(*(_rev(t, d) for t in (cq, ck[:, :, d], cv, clf[:, :, d])), s0, ctx_out)
        ox, _ = gla_chunked(*(_rev(t, d) for t in (xq, xk[:, :, d], xv, xlf[:, :, d])), s_ctx, True)
        o_x = o_x + _rev(ox, d)
        if ctx_out:
            o_c = o_c + _rev(oc, d)
    return o_x, (o_c if ctx_out else None)


def hybrid_mixer(hx, hc, w_in, dn_conv, dn_a_log, dn_dt_bias, dn_norm_g, lb, hg_norm_g,
                 w_br_dn, w_br_hg, w_out, ctx_out):
    bsz, n_lat = hx.shape[:2]
    n_ctx = hc.shape[1]
    proj = jnp.concatenate([hc, hx], axis=1) @ w_in
    pc = jnp.split(proj[:, :n_ctx], IN_SPLITS, axis=-1)
    px = jnp.split(proj[:, n_ctx:], IN_SPLITS, axis=-1)
    odn_x, odn_c = _deltanet_scans(_dn_inputs(px[0], px[2], px[3], dn_conv, dn_a_log, dn_dt_bias),
                                   _dn_inputs(pc[0], pc[2], pc[3], dn_conv, dn_a_log, dn_dt_bias), ctx_out)
    ohg_x, ohg_c = _hgrn2_scans(_hg_inputs(*(_to_col_major(t) for t in (px[4], px[5], px[6])), lb),
                                _hg_inputs(pc[4], pc[5], pc[6], lb), ctx_out)
    ohg_x = _from_col_major(ohg_x.reshape(bsz, n_lat, HG_V)).reshape(bsz, n_lat, HG_HEADS, HG_DV)

    def merge(parts, odn, ohg):
        br_dn = _gated_rmsnorm(odn, parts[1], dn_norm_g) @ w_br_dn
        br_hg = _gated_rmsnorm(ohg, parts[7], hg_norm_g) @ w_br_hg
        gate_dn, gate_hg = jnp.split(parts[8], 2, axis=-1)
        return (jax.nn.sigmoid(gate_dn) * br_dn + jax.nn.sigmoid(gate_hg) * br_hg) @ w_out

    y_x = merge(px, odn_x, ohg_x)
    y_c = merge(pc, odn_c, ohg_c) if ctx_out else None
    return y_x, y_c


def hier_moe(h, w_rg, b_rg, w_re, b_re, w_gate, w_up, w_down):
    n_tok, d = h.shape
    grp_logits = (h @ w_rg + b_rg).astype(F32)
    grp_sel = jnp.argmax(grp_logits, axis=-1)
    grp_p = jnp.take_along_axis(jax.nn.softmax(grp_logits, axis=-1), grp_sel[:, None], axis=-1)
    exp_logits = (h @ w_re + b_re).astype(F32).reshape(n_tok, N_GROUPS, EXPERTS_PER_GROUP)
    in_logits = jnp.take_along_axis(exp_logits, grp_sel[:, None, None], axis=1)[:, 0]
    top_p, top_j = lax.top_k(jax.nn.softmax(in_logits, axis=-1), TOP_K_IN_GROUP)
    gate = grp_p * top_p / jnp.sum(top_p, axis=-1, keepdims=True)
    expert = grp_sel[:, None] * EXPERTS_PER_GROUP + top_j
    n_asg = n_tok * TOP_K_IN_GROUP
    e_flat = expert.reshape(n_asg)
    order = jnp.argsort(e_flat)
    e_sorted = e_flat[order]
    tok_sorted = order // TOP_K_IN_GROUP
    gate_sorted = gate.reshape(n_asg)[order]
    counts = jnp.bincount(e_flat, length=N_EXPERTS)
    padded = (counts + MOE_BLOCK - 1) // MOE_BLOCK * MOE_BLOCK
    pad_end = jnp.cumsum(padded)
    pad_start = pad_end - padded
    start = jnp.cumsum(counts) - counts
    dest = pad_start[e_sorted] + jnp.arange(n_asg) - start[e_sorted]
    n_blocks = (n_asg + N_EXPERTS * (MOE_BLOCK - 1) + MOE_BLOCK - 1) // MOE_BLOCK
    rows = jnp.zeros((n_blocks * MOE_BLOCK, d), h.dtype).at[dest].set(h[tok_sorted])
    block_expert = jnp.minimum(jnp.searchsorted(pad_end, jnp.arange(n_blocks) * MOE_BLOCK, side='right'),
                               N_EXPERTS - 1)

    def expert_block(args):
        xb, e = args
        return (jax.nn.silu(xb @ w_gate[e]) * (xb @ w_up[e])) @ w_down[e]

    out = lax.map(expert_block, (rows.reshape(n_blocks, MOE_BLOCK, d), block_expert)).reshape(-1, d)
    contrib = (out[dest] * gate_sorted[:, None]).astype(h.dtype)
    return jnp.zeros((n_tok, d), h.dtype).at[tok_sorted].add(contrib)


def setup_inputs(seed: int = 0) -> dict:
    key = jax.random.key(seed)
    ks = iter(jax.random.split(key, 26))
    D, L = D_MODEL, DEPTH

    def nrm(shape, scale):
        return jax.random.normal(next(ks), shape, F32) * scale

    x = nrm((BATCH, SEQ, D), 1.0)
    c = nrm((BATCH, D), 1.0)
    ctx = nrm((BATCH, CTX_LEN, D), 1.0)
    c_ctx = nrm((D,), 1.0)
    w_ada = nrm((L, D, 6 * D), 0.5 * D ** -0.5)
    b_ada = nrm((L, 6 * D), 0.02)
    g_mix = 1.0 + nrm((L, D), 0.02)
    g_ffn = 1.0 + nrm((L, D), 0.02)
    g_final = 1.0 + nrm((D,), 0.02)
    w_in = nrm((L, D, N_IN), D ** -0.5)
    dn_conv = nrm((L, CONV_K, 2 * DN_QK + DN_V), CONV_K ** -0.5)
    dn_a_log = jnp.log(jax.random.uniform(next(ks), (L, 2, DN_HEADS), F32, 1.0, 16.0))
    dt = jnp.exp(jax.random.uniform(next(ks), (L, 2, DN_HEADS), F32, math.log(1e-3), math.log(1e-1)))
    dn_dt_bias = dt + jnp.log(-jnp.expm1(-dt))
    dn_norm_g = 1.0 + nrm((L, DN_DV), 0.02)
    hg_lb_logits = nrm((L, HG_K), 0.5)
    hg_norm_g = 1.0 + nrm((L, HG_DV), 0.02)
    w_br_dn = nrm((L, DN_V, D), DN_V ** -0.5)
    w_br_hg = nrm((L, HG_V, D), HG_V ** -0.5)
    w_out = nrm((L, D, D), D ** -0.5)
    w_router_grp = nrm((L, D, N_GROUPS), D ** -0.5)
    b_router_grp = nrm((L, N_GROUPS), 0.01)
    w_router_exp = nrm((L, D, N_EXPERTS), D ** -0.5)
    b_router_exp = nrm((L, N_EXPERTS), 0.01)
    w_exp_gate = nrm((L, N_EXPERTS, D, D_EXPERT), D ** -0.5)
    w_exp_up = nrm((L, N_EXPERTS, D, D_EXPERT), D ** -0.5)
    w_exp_down = nrm((L, N_EXPERTS, D_EXPERT, D), D_EXPERT ** -0.5)
    return {'x': x, 'c': c, 'ctx': ctx, 'c_ctx': c_ctx, 'w_ada': w_ada, 'b_ada': b_ada,
            'g_mix': g_mix, 'g_ffn': g_ffn, 'g_final': g_final, 'w_in': w_in, 'dn_conv': dn_conv,
            'dn_a_log': dn_a_log, 'dn_dt_bias': dn_dt_bias, 'dn_norm_g': dn_norm_g,
            'hg_lb_logits': hg_lb_logits, 'hg_norm_g': hg_norm_g, 'w_br_dn': w_br_dn,
            'w_br_hg': w_br_hg, 'w_out': w_out, 'w_router_grp': w_router_grp,
            'b_router_grp': b_router_grp, 'w_router_exp': w_router_exp, 'b_router_exp': b_router_exp,
            'w_exp_gate': w_exp_gate, 'w_exp_up': w_exp_up, 'w_exp_down': w_exp_down}


def reference(x, c, ctx, c_ctx, w_ada, b_ada, g_mix, g_ffn, g_final, w_in, dn_conv, dn_a_log,
              dn_dt_bias, dn_norm_g, hg_lb_logits, hg_norm_g, w_br_dn, w_br_hg, w_out,
              w_router_grp, b_router_grp, w_router_exp, b_router_exp, w_exp_gate, w_exp_up,
              w_exp_down):
    bsz, n_lat, d = x.shape
    n_ctx = ctx.shape[1]
    lb_w = jax.nn.softmax(hg_lb_logits.astype(F32), axis=0)
    lower_bounds = jnp.cumsum(lb_w, axis=0) - lb_w[0]
    s_c = jax.nn.silu(c)
    s_cc = jax.nn.silu(c_ctx)
    h_ctx = ctx
    for layer in range(DEPTH):
        last = layer == DEPTH - 1
        mod_x = jnp.split((s_c @ w_ada[layer] + b_ada[layer])[:, None, :], 6, axis=-1)
        mod_c = jnp.split(s_cc @ w_ada[layer] + b_ada[layer], 6, axis=-1)
        hx = _rmsnorm(x, g_mix[layer]) * (1.0 + mod_x[1]) + mod_x[0]
        hc = _rmsnorm(h_ctx, g_mix[layer]) * (1.0 + mod_c[1]) + mod_c[0]
        y_x, y_c = hybrid_mixer(hx, hc, w_in[layer], dn_conv[layer], dn_a_log[layer], dn_dt_bias[layer],
                                dn_norm_g[layer], lower_bounds[layer].reshape(HG_HEADS, HG_DK),
                                hg_norm_g[layer], w_br_dn[layer], w_br_hg[layer], w_out[layer],
                                not last)
        x = x + mod_x[2] * y_x
        hx = (_rmsnorm(x, g_ffn[layer]) * (1.0 + mod_x[4]) + mod_x[3]).reshape(-1, d)
        moe_args = (w_router_grp[layer], b_router_grp[layer], w_router_exp[layer], b_router_exp[layer],
                    w_exp_gate[layer], w_exp_up[layer], w_exp_down[layer])
        if last:
            f_x = hier_moe(hx, *moe_args)
        else:
            h_ctx = h_ctx + mod_c[2] * y_c
            hc = (_rmsnorm(h_ctx, g_ffn[layer]) * (1.0 + mod_c[4]) + mod_c[3]).reshape(-1, d)
            f_all = hier_moe(jnp.concatenate([hc, hx], axis=0), *moe_args)
            h_ctx = h_ctx + mod_c[5] * f_all[:bsz * n_ctx].reshape(bsz, n_ctx, d)
            f_x = f_all[bsz * n_ctx:]
        x = x + mod_x[5] * f_x.reshape(bsz, n_lat, d)
    return _rmsnorm(x, g_final)
```

```python
import functools

import jax
import jax.numpy as jnp
from jax import lax
from jax.experimental import pallas as pl
from jax.experimental.pallas import tpu as pltpu

F32 = jnp.float32
BF16 = jnp.bfloat16
HI = lax.Precision.HIGHEST

GRID_W = 64
CHUNK = 64
EPS = 1e-6
F_TINY = 1e-30
HEADS = 4
DK = 128
N_GROUPS = 4
EXPERTS_PER_GROUP = 8
N_EXPERTS = 32
TOP_K = 2
MOE_BLOCK = 128
SUB = 16
EXP_CLAMP = 60.0

C_GATES = 0
C_DNQKV = 2048
C_DNZ = 3584
C_HGQ = 4096
C_HGF = 4608
C_HGI = 5632
C_HGG = 6144
N_MAIN = 6656

VMEM_LIMIT = 56 * 1024 * 1024


def _cparams(sem):
    return pltpu.CompilerParams(dimension_semantics=sem, vmem_limit_bytes=VMEM_LIMIT)


def _silu(x):
    return x * jax.nn.sigmoid(x)


def _softplus(x):
    return jnp.maximum(x, 0.0) + jnp.log1p(jnp.exp(-jnp.abs(x)))


def _dot(a, b, prec=None):
    return jnp.dot(a, b, precision=prec, preferred_element_type=F32)


def _dot_nt(a, b, prec=None):
    return lax.dot_general(a, b, (((1,), (1,)), ((), ())), precision=prec,
                           preferred_element_type=F32)


def _dot_tn(a, b, prec=None):
    return lax.dot_general(a, b, (((0,), (0,)), ((), ())), precision=prec,
                           preferred_element_type=F32)


def _bdot(a, b):
    return _dot(a.astype(BF16), b.astype(BF16))


def _bdot_nt(a, b):
    return _dot_nt(a.astype(BF16), b.astype(BF16))


def _bdot_tn(a, b):
    return _dot_tn(a.astype(BF16), b.astype(BF16))


def _mod_kernel(cc_ref, w_ref, b_ref, o_ref):
    s = _silu(cc_ref[...])
    o_ref[0] = _dot(s, w_ref[0], HI) + b_ref[0]


def _modulation(cc, w_ada, b_ada):
    depth, d, n6 = w_ada.shape
    tn = n6 // 4
    return pl.pallas_call(
        _mod_kernel,
        grid=(depth, n6 // tn),
        in_specs=[pl.BlockSpec((16, d), lambda l, j: (0, 0)),
                  pl.BlockSpec((1, d, tn), lambda l, j: (l, 0, j)),
                  pl.BlockSpec((1, 1, tn), lambda l, j: (l, 0, j))],
        out_specs=pl.BlockSpec((1, 16, tn), lambda l, j: (l, 0, j)),
        out_shape=jax.ShapeDtypeStruct((depth, 16, n6), F32),
        compiler_params=_cparams(("parallel", "parallel")),
        name="modulation",
    )(cc, w_ada, b_ada.reshape(depth, 1, n6))


def _proj_kernel(x_ref, mod_ref, g_ref, w_ref, ws_ref, p_ref, s_ref):
    d = x_ref.shape[-1]
    x = x_ref[0]
    m = mod_ref[0]
    h = x * lax.rsqrt(jnp.mean(x * x, axis=-1, keepdims=True) + EPS) * g_ref[...]
    h = h * (1.0 + m[:, d:2 * d]) + m[:, :d]
    p_ref[0] = _dot(h.astype(BF16), w_ref[...])
    s_ref[0] = _dot_nt(ws_ref[...], h, HI)


def _projection(xc, modsel, g, w_main, w_small_t, n_ctx_tiles, tm):
    bsz, n, d = xc.shape
    n_main = w_main.shape[1]
    return pl.pallas_call(
        _proj_kernel,
        grid=(bsz, n // tm),
        in_specs=[pl.BlockSpec((1, tm, d), lambda b, i: (b, i, 0)),
                  pl.BlockSpec((1, 1, modsel.shape[-1]),
                               lambda b, i: (2 * b + (i >= n_ctx_tiles).astype(jnp.int32), 0, 0)),
                  pl.BlockSpec((1, d), lambda b, i: (0, 0)),
                  pl.BlockSpec((d, n_main), lambda b, i: (0, 0), pipeline_mode=pl.Buffered(1)),
                  pl.BlockSpec((32, d), lambda b, i: (0, 0))],
        out_specs=[pl.BlockSpec((1, tm, n_main), lambda b, i: (b, i, 0)),
                   pl.BlockSpec((1, 32, tm), lambda b, i: (b, 0, i))],
        out_shape=[jax.ShapeDtypeStruct((bsz, n, n_main), F32),
                   jax.ShapeDtypeStruct((bsz, 32, n), F32)],
        compiler_params=_cparams(("parallel", "parallel")),
        name="projection",
    )(xc, modsel, g, w_main, w_small_t)


def _tri_masks(fwd):
    ii = lax.broadcasted_iota(jnp.int32, (CHUNK, CHUNK), 0)
    jj = lax.broadcasted_iota(jnp.int32, (CHUNK, CHUNK), 1)
    if fwd:
        return ii >= jj, ii > jj
    return ii <= jj, ii < jj


def _unit_tri_solve(a, rhs, prec):
    ii = lax.broadcasted_iota(jnp.int32, (CHUNK, CHUNK), 0)
    jj = lax.broadcasted_iota(jnp.int32, (CHUNK, CHUNK), 1)
    same = (ii // SUB) == (jj // SUB)
    dm = jnp.where(same, a, 0.0)
    lm = jnp.where(same, 0.0, a)
    eye = jnp.where(ii == jj, 1.0, 0.0)
    mm = functools.partial(_dot, prec=prec)
    d2 = mm(dm, dm)
    d4 = mm(d2, d2)
    d8 = mm(d4, d4)
    p = eye - dm
    p = p + mm(p, d2)
    p = p + mm(p, d4)
    td = p + mm(p, d8)
    m = mm(td, lm)
    m2 = mm(m, m)
    t1 = mm(td, rhs)
    t2 = t1 + mm(m2, t1)
    return t2 - mm(m, t2)


def _chunk_schedule(s, n_ctx_chunks, n_chunks):
    cf = s
    cb = jnp.where(s < n_ctx_chunks, n_ctx_chunks - 1 - s, n_chunks + n_ctx_chunks - 1 - s)
    return pl.multiple_of(cf * CHUNK, CHUNK), pl.multiple_of(cb * CHUNK, CHUNK)


def _seq_cumsum(x, fwd):
    n = x.shape[0]
    pos = lax.broadcasted_iota(jnp.int32, x.shape, 0) % CHUNK
    s = 1
    while s < CHUNK:
        if fwd:
            x = x + jnp.where(pos >= s, pltpu.roll(x, s, 0), 0.0)
        else:
            x = x + jnp.where(pos < CHUNK - s, pltpu.roll(x, n - s, 0), 0.0)
        s *= 2
    return x


def _dn_chunk(state, q, k, v, beta, gc, fwd, prec):
    incl, strict = _tri_masks(fwd)
    gci = gc[:, :CHUNK]
    gcj = gc.T[:CHUNK, :]
    dec = jnp.where(incl, jnp.exp(jnp.where(incl, gci - gcj, 0.0)), 0.0)
    kb = k * beta
    a = jnp.where(strict, _bdot_nt(kb, k) * dec, 0.0)
    eg = jnp.exp(gc)
    rhs = jnp.concatenate([v * beta, kb * eg], axis=1)
    sol = _unit_tri_solve(a, rhs, prec)
    u, w = sol[:, :DK], sol[:, DK:]
    v_new = u - _bdot(w, state)
    g_tot = gc[CHUNK - 1:CHUNK, :] if fwd else gc[0:1, :]
    kd = k * jnp.exp(g_tot - gc)
    new_state = state * jnp.exp(g_tot) + _bdot_tn(kd, v_new)
    att = jnp.where(incl, _bdot_nt(q, k) * dec, 0.0)
    o = _bdot(q * eg, state) + _bdot(att, v_new)
    return o, new_state


def _dn_kernel(q_ref, k_ref, v_ref, cq_ref, ck_ref, cv_ref, sm_ref, alog_ref, dtb_ref,
               o_ref, qs, ks, vs, bfs, bbs, gfs, gbs, *, n_ctx, prec):
    n = q_ref.shape[1]
    row = lax.broadcasted_iota(jnp.int32, (n, DK), 0)
    seg_first = (row == 0) | (row == n_ctx)
    seg_last = (row == n_ctx - 1) | (row == n - 1)

    def conv_act(x_ref, c_ref):
        x = x_ref[0]
        w = c_ref[...]
        xm = jnp.where(seg_first, 0.0, pltpu.roll(x, 1, 0))
        xp = jnp.where(seg_last, 0.0, pltpu.roll(x, n - 1, 0))
        return _silu(xm * w[0:1] + x * w[1:2] + xp * w[2:3])

    def l2n(y):
        return y * lax.rsqrt(jnp.sum(y * y, axis=-1, keepdims=True) + EPS)

    qs[...] = l2n(conv_act(q_ref, cq_ref)) * (DK ** -0.5)
    ks[...] = l2n(conv_act(k_ref, ck_ref))
    vs[...] = conv_act(v_ref, cv_ref)

    sm = sm_ref[0]
    beta = jax.nn.sigmoid(sm)
    g = -jnp.exp(alog_ref[0]) * _softplus(sm + dtb_ref[0])
    pos = lax.broadcasted_iota(jnp.int32, sm.shape, 1) % CHUNK
    pre, suf = g, g
    s = 1
    while s < CHUNK:
        pre = pre + jnp.where(pos >= s, pltpu.roll(pre, s, 1), 0.0)
        suf = suf + jnp.where(pos < CHUNK - s, pltpu.roll(suf, n - s, 1), 0.0)
        s *= 2

    def col(r):
        return jnp.broadcast_to(r, (DK, n)).T

    bfs[...] = col(beta[0:1])
    bbs[...] = col(beta[1:2])
    gfs[...] = col(pre[2:3])
    gbs[...] = col(suf[3:4])
    o_ref[...] = jnp.zeros(o_ref.shape, F32)

    n_chunks = n // CHUNK
    n_ctx_chunks = n_ctx // CHUNK

    def body(s, carry):
        sf, sb = carry
        rf, rb = _chunk_schedule(s, n_ctx_chunks, n_chunks)
        slf = pl.ds(rf, CHUNK)
        slb = pl.ds(rb, CHUNK)
        of, sf = _dn_chunk(sf, qs[slf, :], ks[slf, :], vs[slf, :], bfs[slf, :], gfs[slf, :], True, prec)
        ob, sb = _dn_chunk(sb, qs[slb, :], ks[slb, :], vs[slb, :], bbs[slb, :], gbs[slb, :], False, prec)
        o_ref[0, slf, :] = o_ref[0, slf, :] + of
        o_ref[0, slb, :] = o_ref[0, slb, :] + ob
        return sf, sb

    zero = jnp.zeros((DK, DK), F32)
    lax.fori_loop(0, n_chunks, body, (zero, zero))


def _deltanet(proj, small, conv_w, alog, dtb, n_ctx, prec=HI):
    bsz, n, _ = proj.shape
    qb = C_DNQKV // DK
    seq = lambda off: pl.BlockSpec((1, n, DK), lambda b, h: (b, 0, off + h))
    cw = lambda off: pl.BlockSpec((3, DK), lambda b, h: (0, off + h))
    return pl.pallas_call(
        functools.partial(_dn_kernel, n_ctx=n_ctx, prec=prec),
        grid=(bsz, HEADS),
        in_specs=[seq(qb), seq(qb + HEADS), seq(qb + 2 * HEADS),
                  cw(0), cw(HEADS), cw(2 * HEADS),
                  pl.BlockSpec((1, 8, n), lambda b, h: (b, h, 0)),
                  pl.BlockSpec((1, 8, 1), lambda b, h: (h, 0, 0)),
                  pl.BlockSpec((1, 8, 1), lambda b, h: (h, 0, 0))],
        out_specs=pl.BlockSpec((1, n, DK), lambda b, h: (b, 0, h)),
        out_shape=jax.ShapeDtypeStruct((bsz, n, HEADS * DK), F32),
        scratch_shapes=[pltpu.VMEM((n, DK), F32) for _ in range(7)],
        compiler_params=_cparams(("parallel", "parallel")),
        name="deltanet",
    )(proj, proj, proj, conv_w, conv_w, conv_w, small, alog, dtb)


def _hg_chunk(state_t, q, kf, v, gc, fwd):
    incl, _ = _tri_masks(fwd)
    edge = 0 if fwd else SUB - 1
    refs = [gc[SUB * i + edge:SUB * i + edge + 1, :] for i in range(CHUNK // SUB)]
    ref_rows = jnp.concatenate([jnp.broadcast_to(r, (SUB, DK)) for r in refs], axis=0)
    qe = q * jnp.exp(gc - ref_rows)
    blocks = []
    for i, r in enumerate(refs):
        ke = kf * jnp.exp(jnp.minimum(r - gc, EXP_CLAMP))
        blocks.append(_bdot_nt(qe[SUB * i:SUB * (i + 1), :], ke))
    att = jnp.where(incl, jnp.concatenate(blocks, axis=0), 0.0)
    g_tot = gc[CHUNK - 1:CHUNK, :] if fwd else gc[0:1, :]
    kd = kf * jnp.exp(g_tot - gc)
    o = _bdot_nt(q * jnp.exp(gc), state_t) + _bdot(att, v)
    new_state_t = state_t * jnp.exp(g_tot) + _bdot_tn(v, kd)
    return o, new_state_t


def _hg_kernel(q_ref, zf_ref, zb_ref, v_ref, lb_ref, o_ref, qs, kfs, kbs, gfs, gbs,
               *, n_ctx):
    n = q_ref.shape[1]
    lb = lb_ref[...]
    qs[...] = _silu(q_ref[0])

    def gates(z_ref, k_scr, g_scr, fwd):
        z = z_ref[0]
        f = lb + (1.0 - lb) * jax.nn.sigmoid(z)
        k_scr[...] = (1.0 - lb) * jax.nn.sigmoid(-z)
        g_scr[...] = _seq_cumsum(jnp.log(jnp.maximum(f, F_TINY)), fwd)

    gates(zf_ref, kfs, gfs, True)
    gates(zb_ref, kbs, gbs, False)
    o_ref[...] = jnp.zeros(o_ref.shape, F32)

    n_chunks = n // CHUNK
    n_ctx_chunks = n_ctx // CHUNK

    def body(s, carry):
        sf, sb = carry
        rf, rb = _chunk_schedule(s, n_ctx_chunks, n_chunks)
        slf = pl.ds(rf, CHUNK)
        slb = pl.ds(rb, CHUNK)
        of, sf = _hg_chunk(sf, qs[slf, :], kfs[slf, :], v_ref[0, slf, :], gfs[slf, :], True)
        ob, sb = _hg_chunk(sb, qs[slb, :], kbs[slb, :], v_ref[0, slb, :], gbs[slb, :], False)
        o_ref[0, slf, :] = o_ref[0, slf, :] + of
        o_ref[0, slb, :] = o_ref[0, slb, :] + ob
        return sf, sb

    zero = jnp.zeros((DK, DK), F32)
    lax.fori_loop(0, n_chunks, body, (zero, zero))


def _hgrn2(hq, hzf, hzb, hv, lb, n_ctx):
    bsz, n, _ = hq.shape
    seq = pl.BlockSpec((1, n, DK), lambda b, h: (b, 0, h))
    return pl.pallas_call(
        functools.partial(_hg_kernel, n_ctx=n_ctx),
        grid=(bsz, HEADS),
        in_specs=[seq, seq, seq, seq, pl.BlockSpec((1, DK), lambda b, h: (0, h))],
        out_specs=seq,
        out_shape=jax.ShapeDtypeStruct((bsz, n, HEADS * DK), F32),
        scratch_shapes=[pltpu.VMEM((n, DK), F32) for _ in range(5)],
        compiler_params=_cparams(("parallel", "parallel")),
        name="hgrn2",
    )(hq, hzf, hzb, hv, lb)


def _gated_rmsnorm(o, z, g):
    parts = []
    for h in range(HEADS):
        oh = o[:, h * DK:(h + 1) * DK]
        zh = z[:, h * DK:(h + 1) * DK]
        y = oh * lax.rsqrt(jnp.mean(oh * oh, axis=-1, keepdims=True) + EPS) * g
        parts.append(y * _silu(zh))
    return jnp.concatenate(parts, axis=1)


def _route(lg):
    lane = lax.broadcasted_iota(jnp.int32, lg.shape, 1).astype(F32)
    neg = -1e30
    big = 1e9
    is_grp = lane < N_GROUPS
    gl = jnp.where(is_grp, lg, neg)
    gmax = jnp.max(gl, axis=-1, keepdims=True)
    gsel = jnp.min(jnp.where(gl == gmax, lane, big), axis=-1, keepdims=True)
    gp = 1.0 / jnp.sum(jnp.where(is_grp, jnp.exp(gl - gmax), 0.0), axis=-1, keepdims=True)
    lo = N_GROUPS + EXPERTS_PER_GROUP * gsel
    el = jnp.where((lane >= lo) & (lane < lo + EXPERTS_PER_GROUP), lg, neg)
    m1 = jnp.max(el, axis=-1, keepdims=True)
    i1 = jnp.min(jnp.where(el == m1, lane, big), axis=-1, keepdims=True)
    el2 = jnp.where(lane == i1, neg, el)
    m2 = jnp.max(el2, axis=-1, keepdims=True)
    i2 = jnp.min(jnp.where(el2 == m2, lane, big), axis=-1, keepdims=True)
    r = jnp.exp(m2 - m1)
    g1 = gp / (1.0 + r)
    g2 = g1 * r
    out = jnp.where(lane == 0, i1 - N_GROUPS, 0.0)
    out = jnp.where(lane == 1, i2 - N_GROUPS, out)
    out = jnp.where(lane == 2, g1, out)
    return jnp.where(lane == 3, g2, out)


def _merge_kernel(gate_ref, z_ref, hg_ref, odn_ref, ohg_ref, x_ref, mod_ref, wdn_ref, whg_ref,
                  wout_ref, gdn_ref, ghg_ref, gffn_ref, wr_ref, br_ref,
                  xo_ref, h2_ref, rt_ref):
    d = x_ref.shape[-1]
    m = mod_ref[0]
    br_dn = _dot(_gated_rmsnorm(odn_ref[0], z_ref[0], gdn_ref[...]).astype(BF16), wdn_ref[...])
    br_hg = _dot(_gated_rmsnorm(ohg_ref[0], hg_ref[0], ghg_ref[...]).astype(BF16), whg_ref[...])
    gate = gate_ref[0]
    mix = jax.nn.sigmoid(gate[:, :d]) * br_dn + jax.nn.sigmoid(gate[:, d:]) * br_hg
    y = _dot(mix.astype(BF16), wout_ref[...])
    x = x_ref[0] + m[:, 2 * d:3 * d] * y
    xo_ref[0] = x
    h2 = x * lax.rsqrt(jnp.mean(x * x, axis=-1, keepdims=True) + EPS) * gffn_ref[...]
    h2 = h2 * (1.0 + m[:, 4 * d:5 * d]) + m[:, 3 * d:4 * d]
    h2_ref[0] = h2
    rt_ref[0] = _route(_dot(h2, wr_ref[...], HI) + br_ref[...])


def _merge(proj, odn, ohg, xc, modsel, wdn, whg, wout, gdn, ghg, gffn, wr, br, n_ctx_tiles, tm):
    bsz, n, d = xc.shape
    hv = HEADS * DK
    tok = lambda w, j: pl.BlockSpec((1, tm, w), lambda b, i: (b, i, j))
    full = lambda a: pl.BlockSpec(a.shape, lambda b, i: (0,) * a.ndim)
    return pl.pallas_call(
        _merge_kernel,
        grid=(bsz, n // tm),
        in_specs=[tok(2 * d, C_GATES // (2 * d)), tok(hv, C_DNZ // hv), tok(hv, C_HGG // hv),
                  tok(hv, 0), tok(hv, 0), tok(d, 0),
                  pl.BlockSpec((1, 1, modsel.shape[-1]),
                               lambda b, i: (2 * b + (i >= n_ctx_tiles).astype(jnp.int32), 0, 0)),
                  full(wdn), full(whg), full(wout), full(gdn), full(ghg), full(gffn),
                  full(wr), full(br)],
        out_specs=[tok(d, 0), tok(d, 0), tok(128, 0)],
        out_shape=[jax.ShapeDtypeStruct((bsz, n, d), F32),
                   jax.ShapeDtypeStruct((bsz, n, d), F32),
                   jax.ShapeDtypeStruct((bsz, n, 128), F32)],
        compiler_params=_cparams(("parallel", "parallel")),
        name="merge",
    )(proj, proj, proj, odn, ohg, xc, modsel, wdn, whg, wout, gdn, ghg, gffn, wr, br)


def _dispatch_kernel(dest_ref, h_ref, init_ref, rows_ref, sem, *, tile):
    del init_ref
    base = pl.program_id(0) * tile

    def row_copy(t, d):
        return pltpu.make_async_copy(h_ref.at[pl.ds(t, 1)], rows_ref.at[pl.ds(d, 1)], sem)

    def issue(r, c):
        t = base + r
        row_copy(t, dest_ref[TOP_K * t]).start()
        row_copy(t, dest_ref[TOP_K * t + 1]).start()
        return c

    lax.fori_loop(0, tile, issue, 0)

    def drain(r, c):
        row_copy(0, 0).wait()
        row_copy(0, 0).wait()
        return c

    lax.fori_loop(0, tile, drain, 0)


def _dispatch(dest, h2, n_rows, tile):
    n_tok, d = h2.shape
    return pl.pallas_call(
        functools.partial(_dispatch_kernel, tile=tile),
        grid_spec=pltpu.PrefetchScalarGridSpec(
            num_scalar_prefetch=1,
            grid=(n_tok // tile,),
            in_specs=[pl.BlockSpec(memory_space=pl.ANY), pl.BlockSpec(memory_space=pl.ANY)],
            out_specs=pl.BlockSpec(memory_space=pl.ANY),
            scratch_shapes=[pltpu.SemaphoreType.DMA(())]),
        out_shape=jax.ShapeDtypeStruct((n_rows, d), F32),
        input_output_aliases={2: 0},
        compiler_params=pltpu.CompilerParams(dimension_semantics=("arbitrary",),
                                             has_side_effects=True),
        name="moe_dispatch",
    )(dest, h2, jnp.zeros((n_rows, d), F32))


def _expert_kernel(be_ref, first_ref, nused_ref, x_ref, wg_ref, wu_ref, wd_ref, o_ref,
                   wg_s, wu_s, wd_s):
    i = pl.program_id(0)

    @pl.when(first_ref[i] == 1)
    def _():
        wg_s[...] = wg_ref[0].astype(BF16)
        wu_s[...] = wu_ref[0].astype(BF16)
        wd_s[...] = wd_ref[0].astype(BF16)

    @pl.when(i < nused_ref[0])
    def _():
        x = x_ref[...].astype(BF16)
        a = _silu(_dot(x, wg_s[...])) * _dot(x, wu_s[...])
        o_ref[...] = _dot(a.astype(BF16), wd_s[...])

    @pl.when(i >= nused_ref[0])
    def _():
        o_ref[...] = jnp.zeros(o_ref.shape, F32)


def _experts(block_expert, first, n_used, rows, wg, wu, wd):
    n_rows, d = rows.shape
    de = wg.shape[-1]
    return pl.pallas_call(
        _expert_kernel,
        grid_spec=pltpu.PrefetchScalarGridSpec(
            num_scalar_prefetch=3,
            grid=(n_rows // MOE_BLOCK,),
            in_specs=[pl.BlockSpec((MOE_BLOCK, d), lambda i, be, f, nu: (i, 0)),
                      pl.BlockSpec((1, d, de), lambda i, be, f, nu: (be[i], 0, 0)),
                      pl.BlockSpec((1, d, de), lambda i, be, f, nu: (be[i], 0, 0)),
                      pl.BlockSpec((1, de, d), lambda i, be, f, nu: (be[i], 0, 0))],
            out_specs=pl.BlockSpec((MOE_BLOCK, d), lambda i, be, f, nu: (i, 0)),
            scratch_shapes=[pltpu.VMEM((d, de), BF16), pltpu.VMEM((d, de), BF16),
                            pltpu.VMEM((de, d), BF16)]),
        out_shape=jax.ShapeDtypeStruct((n_rows, d), F32),
        compiler_params=_cparams(("arbitrary",)),
        name="moe_experts",
    )(block_expert, first, n_used, rows, wg, wu, wd)


def _combine_kernel(dest_ref, y_ref, rt_ref, x_ref, mod_ref, gfin_ref, o_ref, buf0, buf1, sem,
                    *, tile, final):
    d = x_ref.shape[-1]
    base = (pl.program_id(0) * pl.num_programs(1) + pl.program_id(1)) * tile

    def row_copy(src, buf, r):
        return pltpu.make_async_copy(y_ref.at[pl.ds(src, 1)], buf.at[pl.ds(r, 1)], sem)

    def issue(r, c):
        t = base + r
        row_copy(dest_ref[TOP_K * t], buf0, r).start()
        row_copy(dest_ref[TOP_K * t + 1], buf1, r).start()
        return c

    lax.fori_loop(0, tile, issue, 0)

    def drain(r, c):
        row_copy(0, buf0, 0).wait()
        row_copy(0, buf1, 0).wait()
        return c

    lax.fori_loop(0, tile, drain, 0)

    rt = rt_ref[0]
    f = rt[:, 2:3] * buf0[...] + rt[:, 3:4] * buf1[...]
    x = x_ref[0] + mod_ref[0][:, 5 * d:6 * d] * f
    if final:
        x = x * lax.rsqrt(jnp.mean(x * x, axis=-1, keepdims=True) + EPS) * gfin_ref[...]
    o_ref[0] = x


def _combine(dest, y, route, xc, modsel, g_final, n_ctx_tiles, tm, final):
    bsz, n, d = xc.shape
    tok = lambda w: pl.BlockSpec((1, tm, w), lambda b, i, dr: (b, i, 0))
    return pl.pallas_call(
        functools.partial(_combine_kernel, tile=tm, final=final),
        grid_spec=pltpu.PrefetchScalarGridSpec(
            num_scalar_prefetch=1,
            grid=(bsz, n // tm),
            in_specs=[pl.BlockSpec(memory_space=pl.ANY), tok(128), tok(d),
                      pl.BlockSpec((1, 1, modsel.shape[-1]),
                                   lambda b, i, dr: (2 * b + (i >= n_ctx_tiles).astype(jnp.int32), 0, 0)),
                      pl.BlockSpec((1, d), lambda b, i, dr: (0, 0))],
            out_specs=tok(d),
            scratch_shapes=[pltpu.VMEM((tm, d), F32), pltpu.VMEM((tm, d), F32),
                            pltpu.SemaphoreType.DMA(())]),
        out_shape=jax.ShapeDtypeStruct((bsz, n, d), F32),
        compiler_params=_cparams(("arbitrary", "arbitrary")),
        name="moe_combine",
    )(dest, y, route, xc, modsel, g_final)


def _routing_tables(route, n_tok):
    e_flat = route[:, :TOP_K].astype(jnp.int32).reshape(n_tok * TOP_K)
    onehot = (e_flat[:, None] == jnp.arange(N_EXPERTS, dtype=jnp.int32)[None, :]).astype(jnp.int32)
    csum = jnp.cumsum(onehot, axis=0)
    counts = csum[-1]
    rank = jnp.take_along_axis(csum, e_flat[:, None], axis=1)[:, 0] - 1
    padded = (counts + MOE_BLOCK - 1) // MOE_BLOCK * MOE_BLOCK
    pad_end = jnp.cumsum(padded)
    pad_start = pad_end - padded
    dest = pad_start[e_flat] + rank
    n_blocks = (n_tok * TOP_K + N_EXPERTS * (MOE_BLOCK - 1) + MOE_BLOCK - 1) // MOE_BLOCK
    block_expert = jnp.minimum(
        jnp.searchsorted(pad_end, jnp.arange(n_blocks, dtype=jnp.int32) * MOE_BLOCK, side='right'),
        N_EXPERTS - 1).astype(jnp.int32)
    first = jnp.concatenate([jnp.ones((1,), jnp.int32),
                             (block_expert[1:] != block_expert[:-1]).astype(jnp.int32)])
    n_used = (pad_end[-1] // MOE_BLOCK).astype(jnp.int32).reshape(1)
    return dest.astype(jnp.int32), block_expert, first, n_used, n_blocks


def _to_col_major(t):
    bsz, n, f = t.shape
    rows = n // GRID_W
    return t.reshape(bsz, rows, GRID_W, f).transpose(0, 2, 1, 3).reshape(bsz, n, f)


def _from_col_major(t):
    bsz, n, f = t.shape
    rows = n // GRID_W
    return t.reshape(bsz, GRID_W, rows, f).transpose(0, 2, 1, 3).reshape(bsz, n, f)


def _scan_order(t, n_ctx):
    return jnp.concatenate([t[:, :n_ctx], _to_col_major(t[:, n_ctx:])], axis=1)


def _token_order(t, n_ctx):
    return jnp.concatenate([t[:, :n_ctx], _from_col_major(t[:, n_ctx:])], axis=1)


def kernel(x, c, ctx, c_ctx, w_ada, b_ada, g_mix, g_ffn, g_final, w_in, dn_conv, dn_a_log,
           dn_dt_bias, dn_norm_g, hg_lb_logits, hg_norm_g, w_br_dn, w_br_hg, w_out,
           w_router_grp, b_router_grp, w_router_exp, b_router_exp, w_exp_gate, w_exp_up,
           w_exp_down):
    bsz, n_lat, d = x.shape
    n_ctx = ctx.shape[1]
    depth = w_ada.shape[0]
    n = n_ctx + n_lat
    n_tok = bsz * n
    tm = 256
    n_ctx_tiles = n_ctx // tm
    hk = HEADS * DK

    lb_w = jax.nn.softmax(hg_lb_logits.astype(F32), axis=0)
    lower_bounds = jnp.cumsum(lb_w, axis=0) - lb_w[0]

    cc = jnp.zeros((16, d), F32).at[:bsz].set(c).at[bsz].set(c_ctx)
    mod_all = _modulation(cc, w_ada, b_ada)

    xc = jnp.concatenate([ctx, x], axis=1)

    o_qkv, o_z, o_b, o_a = 0, 3 * hk, 4 * hk, 4 * hk + 2 * HEADS
    o_hq = o_a + 2 * HEADS
    o_hf, o_hi, o_hg, o_gt = o_hq + hk, o_hq + 3 * hk, o_hq + 4 * hk, o_hq + 5 * hk

    for layer in range(depth):
        final = layer == depth - 1
        mod = mod_all[layer]
        modsel = jnp.stack([jnp.broadcast_to(mod[bsz], (bsz, 6 * d)), mod[:bsz]],
                           axis=1).reshape(2 * bsz, 1, 6 * d)
        wl = w_in[layer]
        w_main = jnp.concatenate([wl[:, o_gt:], wl[:, o_qkv:o_b], wl[:, o_hq:o_gt]],
                                 axis=1).astype(BF16)
        wb = wl[:, o_b:o_a].reshape(d, 2, HEADS)
        wa = wl[:, o_a:o_hq].reshape(d, 2, HEADS)
        w_small_t = jnp.concatenate([wb, wa, jnp.zeros((d, 4, HEADS), F32)], axis=1)
        w_small_t = w_small_t.transpose(2, 1, 0).reshape(8 * HEADS, d)
        pad4 = jnp.zeros((HEADS, 4, 1), F32)
        zero2 = jnp.zeros((HEADS, 2, 1), F32)
        alog = jnp.concatenate([zero2, dn_a_log[layer].T[:, :, None], pad4], axis=1)
        dtb = jnp.concatenate([zero2, dn_dt_bias[layer].T[:, :, None], pad4], axis=1)

        proj, small = _projection(xc, modsel, g_mix[layer][None, :], w_main, w_small_t,
                                  n_ctx_tiles, tm)
        odn = _deltanet(proj, small, dn_conv[layer], alog, dtb, n_ctx)

        hq = _scan_order(proj[:, :, C_HGQ:C_HGQ + hk], n_ctx)
        hzf = _scan_order(proj[:, :, C_HGF:C_HGF + hk], n_ctx)
        hzb = _scan_order(proj[:, :, C_HGF + hk:C_HGF + 2 * hk], n_ctx)
        hv = _scan_order(proj[:, :, C_HGI:C_HGI + hk], n_ctx)
        ohg = _token_order(_hgrn2(hq, hzf, hzb, hv, lower_bounds[layer][None, :], n_ctx), n_ctx)

        wr = jnp.zeros((d, 128), F32).at[:, :N_GROUPS].set(w_router_grp[layer])
        wr = wr.at[:, N_GROUPS:N_GROUPS + N_EXPERTS].set(w_router_exp[layer])
        br = jnp.zeros((1, 128), F32).at[0, :N_GROUPS].set(b_router_grp[layer])
        br = br.at[0, N_GROUPS:N_GROUPS + N_EXPERTS].set(b_router_exp[layer])
        x_mid, h2, route = _merge(proj, odn, ohg, xc, modsel,
                                  w_br_dn[layer].astype(BF16), w_br_hg[layer].astype(BF16),
                                  w_out[layer].astype(BF16), dn_norm_g[layer][None, :],
                                  hg_norm_g[layer][None, :], g_ffn[layer][None, :], wr, br,
                                  n_ctx_tiles, tm)

        dest, block_expert, first, n_used, n_blocks = _routing_tables(
            route.reshape(n_tok, 128), n_tok)
        rows = _dispatch(dest, h2.reshape(n_tok, d), n_blocks * MOE_BLOCK, tm)
        y = _experts(block_expert, first, n_used, rows, w_exp_gate[layer], w_exp_up[layer],
                     w_exp_down[layer])
        xc = _combine(dest, y, route, x_mid, modsel, g_final[None, :], n_ctx_tiles, tm, final)

    return xc[:, n_ctx:]
```

```python
import functools

import jax
import jax.numpy as jnp
from jax import lax
from jax.experimental import pallas as pl
from jax.experimental.pallas import tpu as pltpu

F32 = jnp.float32
BF16 = jnp.bfloat16
HI = lax.Precision.HIGHEST

GRID_W = 64
CHUNK = 64
EPS = 1e-6
F_TINY = 1e-30
HEADS = 4
DK = 128
N_GROUPS = 4
EXPERTS_PER_GROUP = 8
N_EXPERTS = 32
TOP_K = 2
MOE_BLOCK = 128
SUB = 16
EXP_CLAMP = 60.0
DN_INTRA_UNROLL = 3

C_GATES = 0
C_DNQKV = 2048
C_DNZ = 3584
C_HGQ = 4096
C_HGF = 4608
C_HGI = 5632
C_HGG = 6144
N_MAIN = 6656

VMEM_LIMIT = 56 * 1024 * 1024


def _cparams(sem):
    return pltpu.CompilerParams(dimension_semantics=sem, vmem_limit_bytes=VMEM_LIMIT)


def _silu(x):
    return x * jax.nn.sigmoid(x)


def _softplus(x):
    return jnp.maximum(x, 0.0) + jnp.log1p(jnp.exp(-jnp.abs(x)))


def _dot(a, b, prec=None):
    return jnp.dot(a, b, precision=prec, preferred_element_type=F32)


def _dot_nt(a, b, prec=None):
    return lax.dot_general(a, b, (((1,), (1,)), ((), ())), precision=prec,
                           preferred_element_type=F32)


def _dot_tn(a, b, prec=None):
    return lax.dot_general(a, b, (((0,), (0,)), ((), ())), precision=prec,
                           preferred_element_type=F32)


def _bdot(a, b):
    return _dot(a.astype(BF16), b.astype(BF16))


def _bdot_nt(a, b):
    return _dot_nt(a.astype(BF16), b.astype(BF16))


def _bdot_tn(a, b):
    return _dot_tn(a.astype(BF16), b.astype(BF16))


def _mod_kernel(cc_ref, w_ref, b_ref, o_ref):
    s = _silu(cc_ref[...])
    o_ref[0] = _dot(s, w_ref[0], HI) + b_ref[0]


def _modulation(cc, w_ada, b_ada):
    depth, d, n6 = w_ada.shape
    tn = n6 // 4
    return pl.pallas_call(
        _mod_kernel,
        grid=(depth, n6 // tn),
        in_specs=[pl.BlockSpec((16, d), lambda l, j: (0, 0)),
                  pl.BlockSpec((1, d, tn), lambda l, j: (l, 0, j)),
                  pl.BlockSpec((1, 1, tn), lambda l, j: (l, 0, j))],
        out_specs=pl.BlockSpec((1, 16, tn), lambda l, j: (l, 0, j)),
        out_shape=jax.ShapeDtypeStruct((depth, 16, n6), F32),
        compiler_params=_cparams(("parallel", "parallel")),
        name="modulation",
    )(cc, w_ada, b_ada.reshape(depth, 1, n6))


def _proj_kernel(x_ref, mod_ref, g_ref, w_ref, ws_ref, p_ref, s_ref):
    d = x_ref.shape[-1]
    x = x_ref[0]
    m = mod_ref[0]
    h = x * lax.rsqrt(jnp.mean(x * x, axis=-1, keepdims=True) + EPS) * g_ref[...]
    h = h * (1.0 + m[:, d:2 * d]) + m[:, :d]
    p_ref[0] = _dot(h.astype(BF16), w_ref[...])
    s_ref[0] = _dot_nt(ws_ref[...], h, HI)


def _projection(xc, modsel, g, w_main, w_small_t, n_ctx_tiles, tm):
    bsz, n, d = xc.shape
    n_main = w_main.shape[1]
    return pl.pallas_call(
        _proj_kernel,
        grid=(bsz, n // tm),
        in_specs=[pl.BlockSpec((1, tm, d), lambda b, i: (b, i, 0)),
                  pl.BlockSpec((1, 1, modsel.shape[-1]),
                               lambda b, i: (2 * b + (i >= n_ctx_tiles).astype(jnp.int32), 0, 0)),
                  pl.BlockSpec((1, d), lambda b, i: (0, 0)),
                  pl.BlockSpec((d, n_main), lambda b, i: (0, 0), pipeline_mode=pl.Buffered(1)),
                  pl.BlockSpec((32, d), lambda b, i: (0, 0))],
        out_specs=[pl.BlockSpec((1, tm, n_main), lambda b, i: (b, i, 0)),
                   pl.BlockSpec((1, 32, tm), lambda b, i: (b, 0, i))],
        out_shape=[jax.ShapeDtypeStruct((bsz, n, n_main), F32),
                   jax.ShapeDtypeStruct((bsz, 32, n), F32)],
        compiler_params=_cparams(("parallel", "parallel")),
        name="projection",
    )(xc, modsel, g, w_main, w_small_t)


def _tri_masks(fwd):
    ii = lax.broadcasted_iota(jnp.int32, (CHUNK, CHUNK), 0)
    jj = lax.broadcasted_iota(jnp.int32, (CHUNK, CHUNK), 1)
    if fwd:
        return ii >= jj, ii > jj
    return ii <= jj, ii < jj


def _unit_tri_solve(a, rhs, mm):
    ii = lax.broadcasted_iota(jnp.int32, a[0].shape, 0)
    jj = lax.broadcasted_iota(jnp.int32, a[0].shape, 1)
    same = (ii // SUB) == (jj // SUB)
    eye = jnp.where(ii == jj, 1.0, 0.0)
    each = lambda f, *xs: [f(*t) for t in zip(*xs)]
    axpy = lambda x, y: each(lambda xi, yi: xi + mm(xi, yi), x, y)
    dm = each(lambda t: jnp.where(same, t, 0.0), a)
    lm = each(lambda t: jnp.where(same, 0.0, t), a)
    d2 = each(mm, dm, dm)
    d4 = each(mm, d2, d2)
    d8 = each(mm, d4, d4)
    p = axpy(each(lambda t: eye - t, dm), d2)
    p = axpy(p, d4)
    td = axpy(p, d8)
    m = each(mm, td, lm)
    m2 = each(mm, m, m)
    t1 = each(mm, td, rhs)
    t2 = each(lambda x, y: x + mm(y, x), t1, m2)
    return each(lambda x, y: x - mm(y, x), t2, m)


def _chunk_schedule(s, n_ctx_chunks, n_chunks):
    cf = s
    cb = jnp.where(s < n_ctx_chunks, n_ctx_chunks - 1 - s, n_chunks + n_ctx_chunks - 1 - s)
    return pl.multiple_of(cf * CHUNK, CHUNK), pl.multiple_of(cb * CHUNK, CHUNK)


def _seq_cumsum(x, fwd):
    n = x.shape[0]
    pos = lax.broadcasted_iota(jnp.int32, x.shape, 0) % CHUNK
    s = 1
    while s < CHUNK:
        if fwd:
            x = x + jnp.where(pos >= s, pltpu.roll(x, s, 0), 0.0)
        else:
            x = x + jnp.where(pos < CHUNK - s, pltpu.roll(x, n - s, 0), 0.0)
        s *= 2
    return x


def _dn_intra(q, k, v, beta, gc, fwds):
    m = q[0].shape[0]
    ii = lax.broadcasted_iota(jnp.int32, (m, m), 0)
    jj = lax.broadcasted_iota(jnp.int32, (m, m), 1)
    blk = ii // CHUNK
    same = blk == (jj // CHUNK)
    is_fwd = functools.reduce(lambda x, y: x | y,
                              [blk == g for g, f in enumerate(fwds) if f], blk < 0)
    incl = same & ((is_fwd & (ii >= jj)) | (~is_fwd & (ii <= jj)))
    strict = incl & (ii != jj)
    wide = lambda t: jnp.concatenate([t] * (m // DK), axis=1)
    tall = lambda t: jnp.concatenate([t] * (m // DK), axis=0)
    each = lambda f, *xs: [f(*t) for t in zip(*xs)]
    kk = each(_bdot_nt, k, k)
    qk = each(_bdot_nt, q, k)
    dec = each(lambda g: jnp.where(incl, jnp.exp(jnp.where(incl, wide(g) - tall(g.T), 0.0)), 0.0), gc)
    a = each(lambda x, b, d: jnp.where(strict, x * wide(b) * d, 0.0), kk, beta, dec)
    eg = each(jnp.exp, gc)
    rhs = each(lambda vi, ki, b, e: jnp.concatenate([vi * b, ki * (b * e)], axis=1), v, k, beta, eg)
    sol = _unit_tri_solve(a, rhs, _bdot)

    def to_end(g):
        rows = [g[i * CHUNK + (CHUNK - 1 if f else 0)][None, :] for i, f in enumerate(fwds)]
        return jnp.concatenate([jnp.broadcast_to(r, (CHUNK, DK)) for r in rows], axis=0) - g

    n_stack = len(fwds)
    ri = lax.broadcasted_iota(jnp.int32, (n_stack * DK, m), 0) // DK
    ci = lax.broadcasted_iota(jnp.int32, (n_stack * DK, m), 1) // CHUNK
    kdt_bd = each(lambda ki, g: jnp.where(
        ri == ci, jnp.concatenate([(ki * jnp.exp(to_end(g))).T] * n_stack, axis=0), 0.0), k, gc)
    att = each(lambda x, d: jnp.where(incl, x * d, 0.0), qk, dec)
    att_uw = each(_bdot, att, sol)
    kd_uw = each(_bdot, kdt_bd, sol)
    q_eff = each(lambda x, e, aw: x * e - aw[:, DK:], q, eg, att_uw)
    return [(aw[:, :DK], qe, kw[:, :DK], kw[:, DK:]) for aw, qe, kw in zip(att_uw, q_eff, kd_uw)]


def _dn_kernel(q_ref, k_ref, v_ref, cq_ref, ck_ref, cv_ref, sm_ref, alog_ref, dtb_ref,
               o_ref, qs, ks, vs, bfs, bbs, gfs, gbs, qes, cs, ns, *, n_ctx):
    n = q_ref.shape[1]
    row = lax.broadcasted_iota(jnp.int32, (n, DK), 0)
    seg_first = (row == 0) | (row == n_ctx)
    seg_last = (row == n_ctx - 1) | (row == n - 1)

    def conv_act(x_ref, c_ref):
        x = x_ref[0]
        w = c_ref[...]
        xm = jnp.where(seg_first, 0.0, pltpu.roll(x, 1, 0))
        xp = jnp.where(seg_last, 0.0, pltpu.roll(x, n - 1, 0))
        return _silu(xm * w[0:1] + x * w[1:2] + xp * w[2:3])

    def l2n(y):
        return y * lax.rsqrt(jnp.sum(y * y, axis=-1, keepdims=True) + EPS)

    qs[...] = l2n(conv_act(q_ref, cq_ref)) * (DK ** -0.5)
    ks[...] = l2n(conv_act(k_ref, ck_ref))
    vs[...] = conv_act(v_ref, cv_ref)

    sm = sm_ref[0]
    beta = jax.nn.sigmoid(sm)
    g = -jnp.exp(alog_ref[0]) * _softplus(sm + dtb_ref[0])
    pos = lax.broadcasted_iota(jnp.int32, sm.shape, 1) % CHUNK
    pre, suf = g, g
    s = 1
    while s < CHUNK:
        pre = pre + jnp.where(pos >= s, pltpu.roll(pre, s, 1), 0.0)
        suf = suf + jnp.where(pos < CHUNK - s, pltpu.roll(suf, n - s, 1), 0.0)
        s *= 2

    def col(r):
        return jnp.broadcast_to(r, (DK, n)).T

    bfs[...] = col(beta[0:1])
    bbs[...] = col(beta[1:2])
    gfs[...] = col(pre[2:3])
    gbs[...] = col(suf[3:4])

    n_chunks = n // CHUNK
    n_ctx_chunks = n_ctx // CHUNK

    dirs = ((True, bfs, gfs), (False, bbs, gbs))

    def intra(c, carry):
        starts = [pl.multiple_of((c * DN_INTRA_UNROLL + j) * 2 * CHUNK, 2 * CHUNK)
                  for j in range(DN_INTRA_UNROLL)]
        pairs = [pl.ds(r0, 2 * CHUNK) for r0 in starts]
        stack = lambda t: jnp.concatenate([t[:CHUNK], t[:CHUNK], t[CHUNK:], t[CHUNK:]], axis=0)
        both = lambda tf, tb: jnp.concatenate(
            [tf[:CHUNK], tb[:CHUNK], tf[CHUNK:], tb[CHUNK:]], axis=0)
        results = _dn_intra(
            [stack(qs[p, :]) for p in pairs], [stack(ks[p, :]) for p in pairs],
            [stack(vs[p, :]) for p in pairs],
            [both(bfs[p, :], bbs[p, :]) for p in pairs],
            [both(gfs[p, :], gbs[p, :]) for p in pairs], (True, False, True, False))
        for r0, (o0, q_eff, c_all, n_all) in zip(starts, results):
            for g in range(4):
                di = g % 2
                rg = r0 + (g // 2) * CHUNK
                sl = pl.ds(rg, CHUNK)
                sl2 = pl.ds(pl.multiple_of(2 * rg, 2 * CHUNK), DK)
                if di == 0:
                    o_ref[0, sl, :] = o0[g * CHUNK:(g + 1) * CHUNK] + o0[(g + 1) * CHUNK:(g + 2) * CHUNK]
                qes[di, sl, :] = q_eff[g * CHUNK:(g + 1) * CHUNK].astype(BF16)
                cs[di, sl2, :] = c_all[g * DK:(g + 1) * DK]
                ns[di, sl2, :] = n_all[g * DK:(g + 1) * DK].astype(BF16)
        return carry

    lax.fori_loop(0, n_chunks // (2 * DN_INTRA_UNROLL), intra, 0)

    def step(s, states):
        rows = _chunk_schedule(s, n_ctx_chunks, n_chunks)
        sls = [pl.ds(r0, CHUNK) for r0 in rows]
        sl2s = [pl.ds(pl.multiple_of(2 * r0, 2 * CHUNK), DK) for r0 in rows]
        stb = [st.astype(BF16) for st in states]
        corr = [_dot(ns[di, sl2s[di], :], stb[di]) for di in range(2)]
        outs = [_dot(qes[di, sls[di], :], stb[di]) for di in range(2)]
        new_states = []
        for di, (fwd, _, g_scr) in enumerate(dirs):
            g_tot = g_scr[pl.ds(rows[di] + (CHUNK - 1 if fwd else 0), 1), :]
            new_states.append(states[di] * jnp.exp(g_tot) + cs[di, sl2s[di], :] - corr[di])
            o_ref[0, sls[di], :] = o_ref[0, sls[di], :] + outs[di]
        return tuple(new_states)

    zero = jnp.zeros((DK, DK), F32)
    lax.fori_loop(0, n_chunks, step, (zero, zero))


def _deltanet(proj, small, conv_w, alog, dtb, n_ctx):
    bsz, n, _ = proj.shape
    qb = C_DNQKV // DK
    seq = lambda off: pl.BlockSpec((1, n, DK), lambda b, h: (b, 0, off + h))
    cw = lambda off: pl.BlockSpec((3, DK), lambda b, h: (0, off + h))
    return pl.pallas_call(
        functools.partial(_dn_kernel, n_ctx=n_ctx),
        grid=(bsz, HEADS),
        in_specs=[seq(qb), seq(qb + HEADS), seq(qb + 2 * HEADS),
                  cw(0), cw(HEADS), cw(2 * HEADS),
                  pl.BlockSpec((1, 8, n), lambda b, h: (b, h, 0)),
                  pl.BlockSpec((1, 8, 1), lambda b, h: (h, 0, 0)),
                  pl.BlockSpec((1, 8, 1), lambda b, h: (h, 0, 0))],
        out_specs=pl.BlockSpec((1, n, DK), lambda b, h: (b, 0, h)),
        out_shape=jax.ShapeDtypeStruct((bsz, n, HEADS * DK), F32),
        scratch_shapes=[pltpu.VMEM((n, DK), F32) for _ in range(7)] + [
            pltpu.VMEM((2, n, DK), BF16),
            pltpu.VMEM((2, 2 * n, DK), F32),
            pltpu.VMEM((2, 2 * n, DK), BF16)],
        compiler_params=_cparams(("parallel", "parallel")),
        name="deltanet",
    )(proj, proj, proj, conv_w, conv_w, conv_w, small, alog, dtb)


def _hg_chunk(state_t, q, kf, v, gc, fwd):
    incl, _ = _tri_masks(fwd)
    edge = 0 if fwd else SUB - 1
    refs = [gc[SUB * i + edge:SUB * i + edge + 1, :] for i in range(CHUNK // SUB)]
    ref_rows = jnp.concatenate([jnp.broadcast_to(r, (SUB, DK)) for r in refs], axis=0)
    qe = q * jnp.exp(gc - ref_rows)
    blocks = []
    for i, r in enumerate(refs):
        ke = kf * jnp.exp(jnp.minimum(r - gc, EXP_CLAMP))
        blocks.append(_bdot_nt(qe[SUB * i:SUB * (i + 1), :], ke))
    att = jnp.where(incl, jnp.concatenate(blocks, axis=0), 0.0)
    g_tot = gc[CHUNK - 1:CHUNK, :] if fwd else gc[0:1, :]
    kd = kf * jnp.exp(g_tot - gc)
    o = _bdot_nt(q * jnp.exp(gc), state_t) + _bdot(att, v)
    new_state_t = state_t * jnp.exp(g_tot) + _bdot_tn(v, kd)
    return o, new_state_t


def _hg_kernel(q_ref, zf_ref, zb_ref, v_ref, lb_ref, o_ref, qs, kfs, kbs, gfs, gbs,
               *, n_ctx):
    n = q_ref.shape[1]
    lb = lb_ref[...]
    qs[...] = _silu(q_ref[0])

    def gates(z_ref, k_scr, g_scr, fwd):
        z = z_ref[0]
        f = lb + (1.0 - lb) * jax.nn.sigmoid(z)
        k_scr[...] = (1.0 - lb) * jax.nn.sigmoid(-z)
        g_scr[...] = _seq_cumsum(jnp.log(jnp.maximum(f, F_TINY)), fwd)

    gates(zf_ref, kfs, gfs, True)
    gates(zb_ref, kbs, gbs, False)
    o_ref[...] = jnp.zeros(o_ref.shape, F32)

    n_chunks = n // CHUNK
    n_ctx_chunks = n_ctx // CHUNK

    def body(s, carry):
        sf, sb = carry
        rf, rb = _chunk_schedule(s, n_ctx_chunks, n_chunks)
        slf = pl.ds(rf, CHUNK)
        slb = pl.ds(rb, CHUNK)
        of, sf = _hg_chunk(sf, qs[slf, :], kfs[slf, :], v_ref[0, slf, :], gfs[slf, :], True)
        ob, sb = _hg_chunk(sb, qs[slb, :], kbs[slb, :], v_ref[0, slb, :], gbs[slb, :], False)
        o_ref[0, slf, :] = o_ref[0, slf, :] + of
        o_ref[0, slb, :] = o_ref[0, slb, :] + ob
        return sf, sb

    zero = jnp.zeros((DK, DK), F32)
    lax.fori_loop(0, n_chunks, body, (zero, zero))


def _hgrn2(hq, hzf, hzb, hv, lb, n_ctx):
    bsz, n, _ = hq.shape
    seq = pl.BlockSpec((1, n, DK), lambda b, h: (b, 0, h))
    return pl.pallas_call(
        functools.partial(_hg_kernel, n_ctx=n_ctx),
        grid=(bsz, HEADS),
        in_specs=[seq, seq, seq, seq, pl.BlockSpec((1, DK), lambda b, h: (0, h))],
        out_specs=seq,
        out_shape=jax.ShapeDtypeStruct((bsz, n, HEADS * DK), F32),
        scratch_shapes=[pltpu.VMEM((n, DK), F32) for _ in range(5)],
        compiler_params=_cparams(("parallel", "parallel")),
        name="hgrn2",
    )(hq, hzf, hzb, hv, lb)


def _gated_rmsnorm(o, z, g):
    parts = []
    for h in range(HEADS):
        oh = o[:, h * DK:(h + 1) * DK]
        zh = z[:, h * DK:(h + 1) * DK]
        y = oh * lax.rsqrt(jnp.mean(oh * oh, axis=-1, keepdims=True) + EPS) * g
        parts.append(y * _silu(zh))
    return jnp.concatenate(parts, axis=1)


def _route(lg):
    lane = lax.broadcasted_iota(jnp.int32, lg.shape, 1).astype(F32)
    neg = -1e30
    big = 1e9
    is_grp = lane < N_GROUPS
    gl = jnp.where(is_grp, lg, neg)
    gmax = jnp.max(gl, axis=-1, keepdims=True)
    gsel = jnp.min(jnp.where(gl == gmax, lane, big), axis=-1, keepdims=True)
    gp = 1.0 / jnp.sum(jnp.where(is_grp, jnp.exp(gl - gmax), 0.0), axis=-1, keepdims=True)
    lo = N_GROUPS + EXPERTS_PER_GROUP * gsel
    el = jnp.where((lane >= lo) & (lane < lo + EXPERTS_PER_GROUP), lg, neg)
    m1 = jnp.max(el, axis=-1, keepdims=True)
    i1 = jnp.min(jnp.where(el == m1, lane, big), axis=-1, keepdims=True)
    el2 = jnp.where(lane == i1, neg, el)
    m2 = jnp.max(el2, axis=-1, keepdims=True)
    i2 = jnp.min(jnp.where(el2 == m2, lane, big), axis=-1, keepdims=True)
    r = jnp.exp(m2 - m1)
    g1 = gp / (1.0 + r)
    g2 = g1 * r
    out = jnp.where(lane == 0, i1 - N_GROUPS, 0.0)
    out = jnp.where(lane == 1, i2 - N_GROUPS, out)
    out = jnp.where(lane == 2, g1, out)
    return jnp.where(lane == 3, g2, out)


def _merge_kernel(gate_ref, z_ref, hg_ref, odn_ref, ohg_ref, x_ref, mod_ref, wdn_ref, whg_ref,
                  wout_ref, gdn_ref, ghg_ref, gffn_ref, wr_ref, br_ref,
                  xo_ref, h2_ref, rt_ref, cnt_ref):
    d = x_ref.shape[-1]
    m = mod_ref[0]

    @pl.when((pl.program_id(0) == 0) & (pl.program_id(1) == 0))
    def _():
        cnt_ref[...] = jnp.zeros(cnt_ref.shape, F32)

    br_dn = _dot(_gated_rmsnorm(odn_ref[0], z_ref[0], gdn_ref[...]).astype(BF16), wdn_ref[...])
    br_hg = _dot(_gated_rmsnorm(ohg_ref[0], hg_ref[0], ghg_ref[...]).astype(BF16), whg_ref[...])
    gate = gate_ref[0]
    mix = jax.nn.sigmoid(gate[:, :d]) * br_dn + jax.nn.sigmoid(gate[:, d:]) * br_hg
    y = _dot(mix.astype(BF16), wout_ref[...])
    x = x_ref[0] + m[:, 2 * d:3 * d] * y
    xo_ref[0] = x
    h2 = x * lax.rsqrt(jnp.mean(x * x, axis=-1, keepdims=True) + EPS) * gffn_ref[...]
    h2 = h2 * (1.0 + m[:, 4 * d:5 * d]) + m[:, 3 * d:4 * d]
    h2_ref[0] = h2
    rt = _route(_dot(h2, wr_ref[...], HI) + br_ref[...])
    tm = rt.shape[0]
    lane = lax.broadcasted_iota(jnp.int32, rt.shape, 1)
    lanef = lane.astype(F32)
    oh1 = jnp.where(lanef == rt[:, 0:1], 1.0, 0.0)
    oh2 = jnp.where(lanef == rt[:, 1:2], 1.0, 0.0)
    both = oh1 + oh2
    ii = lax.broadcasted_iota(jnp.int32, (tm, tm), 0)
    jj = lax.broadcasted_iota(jnp.int32, (tm, tm), 1)
    earlier = jnp.where(ii > jj, 1.0, 0.0).astype(BF16)
    prior = _dot(earlier, both.astype(BF16)) + cnt_ref[0:1, :]
    rank1 = jnp.sum(prior * oh1, axis=-1, keepdims=True)
    rank2 = jnp.sum(prior * oh2, axis=-1, keepdims=True)
    rt = jnp.where(lane == 4, rank1, rt)
    rt_ref[0] = jnp.where(lane == 5, rank2, rt)
    cnt_ref[0:1, :] = cnt_ref[0:1, :] + jnp.sum(both, axis=0, keepdims=True)


def _merge(proj, odn, ohg, xc, modsel, wdn, whg, wout, gdn, ghg, gffn, wr, br, n_ctx_tiles, tm):
    bsz, n, d = xc.shape
    hv = HEADS * DK
    tok = lambda w, j: pl.BlockSpec((1, tm, w), lambda b, i: (b, i, j))
    full = lambda a: pl.BlockSpec(a.shape, lambda b, i: (0,) * a.ndim)
    return pl.pallas_call(
        _merge_kernel,
        grid=(bsz, n // tm),
        in_specs=[tok(2 * d, C_GATES // (2 * d)), tok(hv, C_DNZ // hv), tok(hv, C_HGG // hv),
                  tok(hv, 0), tok(hv, 0), tok(d, 0),
                  pl.BlockSpec((1, 1, modsel.shape[-1]),
                               lambda b, i: (2 * b + (i >= n_ctx_tiles).astype(jnp.int32), 0, 0)),
                  full(wdn), full(whg), full(wout), full(gdn), full(ghg), full(gffn),
                  full(wr), full(br)],
        out_specs=[tok(d, 0), tok(d, 0), tok(128, 0),
                   pl.BlockSpec((8, 128), lambda b, i: (0, 0))],
        out_shape=[jax.ShapeDtypeStruct((bsz, n, d), F32),
                   jax.ShapeDtypeStruct((bsz, n, d), F32),
                   jax.ShapeDtypeStruct((bsz, n, 128), F32),
                   jax.ShapeDtypeStruct((8, 128), F32)],
        compiler_params=_cparams(("arbitrary", "arbitrary")),
        name="merge",
    )(proj, proj, proj, odn, ohg, xc, modsel, wdn, whg, wout, gdn, ghg, gffn, wr, br)


def _dispatch_kernel(dest_ref, h_ref, init_ref, rows_ref, sem, *, tile):
    del init_ref
    base = pl.program_id(0) * tile

    def row_copy(r, d):
        return pltpu.make_async_copy(h_ref.at[pl.ds(r, 1)], rows_ref.at[pl.ds(d, 1)], sem)

    def issue(r, c):
        t = base + r
        row_copy(r, dest_ref[TOP_K * t]).start()
        row_copy(r, dest_ref[TOP_K * t + 1]).start()
        return c

    lax.fori_loop(0, tile, issue, 0)

    def drain(r, c):
        row_copy(0, 0).wait()
        row_copy(0, 0).wait()
        return c

    lax.fori_loop(0, tile, drain, 0)


def _dispatch(dest, h2, n_rows, tile):
    n_tok, d = h2.shape
    return pl.pallas_call(
        functools.partial(_dispatch_kernel, tile=tile),
        grid_spec=pltpu.PrefetchScalarGridSpec(
            num_scalar_prefetch=1,
            grid=(n_tok // tile,),
            in_specs=[pl.BlockSpec((tile, d), lambda i, dr: (i, 0)),
                      pl.BlockSpec(memory_space=pl.ANY)],
            out_specs=pl.BlockSpec(memory_space=pl.ANY),
            scratch_shapes=[pltpu.SemaphoreType.DMA(())]),
        out_shape=jax.ShapeDtypeStruct((n_rows, d), F32),
        input_output_aliases={2: 0},
        compiler_params=pltpu.CompilerParams(dimension_semantics=("arbitrary",),
                                             has_side_effects=True),
        name="moe_dispatch",
    )(dest, h2, jnp.zeros((n_rows, d), F32))


def _expert_kernel(be_ref, first_ref, nused_ref, x_ref, wg_ref, wu_ref, wd_ref, o_ref,
                   wg_s, wu_s, wd_s):
    i = pl.program_id(0)

    @pl.when(first_ref[i] == 1)
    def _():
        wg_s[...] = wg_ref[0].astype(BF16)
        wu_s[...] = wu_ref[0].astype(BF16)
        wd_s[...] = wd_ref[0].astype(BF16)

    @pl.when(i < nused_ref[0])
    def _():
        x = x_ref[...].astype(BF16)
        a = _silu(_dot(x, wg_s[...])) * _dot(x, wu_s[...])
        o_ref[...] = _dot(a.astype(BF16), wd_s[...])

    @pl.when(i >= nused_ref[0])
    def _():
        o_ref[...] = jnp.zeros(o_ref.shape, F32)


def _experts(block_expert, first, n_used, rows, wg, wu, wd):
    n_rows, d = rows.shape
    de = wg.shape[-1]
    return pl.pallas_call(
        _expert_kernel,
        grid_spec=pltpu.PrefetchScalarGridSpec(
            num_scalar_prefetch=3,
            grid=(n_rows // MOE_BLOCK,),
            in_specs=[pl.BlockSpec((MOE_BLOCK, d), lambda i, be, f, nu: (i, 0)),
                      pl.BlockSpec((1, d, de), lambda i, be, f, nu: (be[i], 0, 0)),
                      pl.BlockSpec((1, d, de), lambda i, be, f, nu: (be[i], 0, 0)),
                      pl.BlockSpec((1, de, d), lambda i, be, f, nu: (be[i], 0, 0))],
            out_specs=pl.BlockSpec((MOE_BLOCK, d), lambda i, be, f, nu: (i, 0)),
            scratch_shapes=[pltpu.VMEM((d, de), BF16), pltpu.VMEM((d, de), BF16),
                            pltpu.VMEM((de, d), BF16)]),
        out_shape=jax.ShapeDtypeStruct((n_rows, d), F32),
        compiler_params=_cparams(("arbitrary",)),
        name="moe_experts",
    )(block_expert, first, n_used, rows, wg, wu, wd)


def _combine_kernel(dest_ref, y_ref, rt_ref, x_ref, mod_ref, gfin_ref, o_ref, buf0, buf1, sem,
                    *, tile, final):
    d = x_ref.shape[-1]
    base = (pl.program_id(0) * pl.num_programs(1) + pl.program_id(1)) * tile

    def row_copy(src, buf, r):
        return pltpu.make_async_copy(y_ref.at[pl.ds(src, 1)], buf.at[pl.ds(r, 1)], sem)

    def issue(r, c):
        t = base + r
        row_copy(dest_ref[TOP_K * t], buf0, r).start()
        row_copy(dest_ref[TOP_K * t + 1], buf1, r).start()
        return c

    lax.fori_loop(0, tile, issue, 0)

    def drain(r, c):
        row_copy(0, buf0, 0).wait()
        row_copy(0, buf1, 0).wait()
        return c

    lax.fori_loop(0, tile, drain, 0)

    rt = rt_ref[0]
    f = rt[:, 2:3] * buf0[...] + rt[:, 3:4] * buf1[...]
    x = x_ref[0] + mod_ref[0][:, 5 * d:6 * d] * f
    if final:
        x = x * lax.rsqrt(jnp.mean(x * x, axis=-1, keepdims=True) + EPS) * gfin_ref[...]
    o_ref[0] = x


def _combine(dest, y, route, xc, modsel, g_final, n_ctx_tiles, tm, final):
    bsz, n, d = xc.shape
    tok = lambda w: pl.BlockSpec((1, tm, w), lambda b, i, dr: (b, i, 0))
    return pl.pallas_call(
        functools.partial(_combine_kernel, tile=tm, final=final),
        grid_spec=pltpu.PrefetchScalarGridSpec(
            num_scalar_prefetch=1,
            grid=(bsz, n // tm),
            in_specs=[pl.BlockSpec(memory_space=pl.ANY), tok(128), tok(d),
                      pl.BlockSpec((1, 1, modsel.shape[-1]),
                                   lambda b, i, dr: (2 * b + (i >= n_ctx_tiles).astype(jnp.int32), 0, 0)),
                      pl.BlockSpec((1, d), lambda b, i, dr: (0, 0))],
            out_specs=tok(d),
            scratch_shapes=[pltpu.VMEM((tm, d), F32), pltpu.VMEM((tm, d), F32),
                            pltpu.SemaphoreType.DMA(())]),
        out_shape=jax.ShapeDtypeStruct((bsz, n, d), F32),
        compiler_params=_cparams(("arbitrary", "arbitrary")),
        name="moe_combine",
    )(dest, y, route, xc, modsel, g_final)


def _routing_tables(route, counts, n_tok):
    e_flat = route[:, :TOP_K].astype(jnp.int32).reshape(n_tok * TOP_K)
    rank = route[:, 4:4 + TOP_K].astype(jnp.int32).reshape(n_tok * TOP_K)
    counts = counts[0, :N_EXPERTS].astype(jnp.int32)
    padded = (counts + MOE_BLOCK - 1) // MOE_BLOCK * MOE_BLOCK
    pad_end = jnp.cumsum(padded)
    pad_start = pad_end - padded
    dest = pad_start[e_flat] + rank
    n_blocks = (n_tok * TOP_K + N_EXPERTS * (MOE_BLOCK - 1) + MOE_BLOCK - 1) // MOE_BLOCK
    block_row = jnp.arange(n_blocks, dtype=jnp.int32) * MOE_BLOCK
    block_expert = jnp.minimum(
        jnp.sum((pad_end[None, :] <= block_row[:, None]).astype(jnp.int32), axis=1),
        N_EXPERTS - 1).astype(jnp.int32)
    first = jnp.concatenate([jnp.ones((1,), jnp.int32),
                             (block_expert[1:] != block_expert[:-1]).astype(jnp.int32)])
    n_used = (pad_end[-1] // MOE_BLOCK).astype(jnp.int32).reshape(1)
    return dest.astype(jnp.int32), block_expert, first, n_used, n_blocks


def _to_col_major(t):
    bsz, n, f = t.shape
    rows = n // GRID_W
    return t.reshape(bsz, rows, GRID_W, f).transpose(0, 2, 1, 3).reshape(bsz, n, f)


def _from_col_major(t):
    bsz, n, f = t.shape
    rows = n // GRID_W
    return t.reshape(bsz, GRID_W, rows, f).transpose(0, 2, 1, 3).reshape(bsz, n, f)


def _scan_order(t, n_ctx):
    return jnp.concatenate([t[:, :n_ctx], _to_col_major(t[:, n_ctx:])], axis=1)


def _token_order(t, n_ctx):
    return jnp.concatenate([t[:, :n_ctx], _from_col_major(t[:, n_ctx:])], axis=1)


def kernel(x, c, ctx, c_ctx, w_ada, b_ada, g_mix, g_ffn, g_final, w_in, dn_conv, dn_a_log,
           dn_dt_bias, dn_norm_g, hg_lb_logits, hg_norm_g, w_br_dn, w_br_hg, w_out,
           w_router_grp, b_router_grp, w_router_exp, b_router_exp, w_exp_gate, w_exp_up,
           w_exp_down):
    bsz, n_lat, d = x.shape
    n_ctx = ctx.shape[1]
    depth = w_ada.shape[0]
    n = n_ctx + n_lat
    n_tok = bsz * n
    tm = 256
    n_ctx_tiles = n_ctx // tm
    hk = HEADS * DK

    lb_w = jax.nn.softmax(hg_lb_logits.astype(F32), axis=0)
    lower_bounds = jnp.cumsum(lb_w, axis=0) - lb_w[0]

    cc = jnp.zeros((16, d), F32).at[:bsz].set(c).at[bsz].set(c_ctx)
    mod_all = _modulation(cc, w_ada, b_ada)

    xc = jnp.concatenate([ctx, x], axis=1)

    o_qkv, o_z, o_b, o_a = 0, 3 * hk, 4 * hk, 4 * hk + 2 * HEADS
    o_hq = o_a + 2 * HEADS
    o_hf, o_hi, o_hg, o_gt = o_hq + hk, o_hq + 3 * hk, o_hq + 4 * hk, o_hq + 5 * hk

    for layer in range(depth):
        final = layer == depth - 1
        mod = mod_all[layer]
        modsel = jnp.stack([jnp.broadcast_to(mod[bsz], (bsz, 6 * d)), mod[:bsz]],
                           axis=1).reshape(2 * bsz, 1, 6 * d)
        wl = w_in[layer]
        w_main = jnp.concatenate([wl[:, o_gt:], wl[:, o_qkv:o_b], wl[:, o_hq:o_gt]],
                                 axis=1).astype(BF16)
        wb = wl[:, o_b:o_a].reshape(d, 2, HEADS)
        wa = wl[:, o_a:o_hq].reshape(d, 2, HEADS)
        w_small_t = jnp.concatenate([wb, wa, jnp.zeros((d, 4, HEADS), F32)], axis=1)
        w_small_t = w_small_t.transpose(2, 1, 0).reshape(8 * HEADS, d)
        pad4 = jnp.zeros((HEADS, 4, 1), F32)
        zero2 = jnp.zeros((HEADS, 2, 1), F32)
        alog = jnp.concatenate([zero2, dn_a_log[layer].T[:, :, None], pad4], axis=1)
        dtb = jnp.concatenate([zero2, dn_dt_bias[layer].T[:, :, None], pad4], axis=1)

        proj, small = _projection(xc, modsel, g_mix[layer][None, :], w_main, w_small_t,
                                  n_ctx_tiles, tm)
        odn = _deltanet(proj, small, dn_conv[layer], alog, dtb, n_ctx)

        hq = _scan_order(proj[:, :, C_HGQ:C_HGQ + hk], n_ctx)
        hzf = _scan_order(proj[:, :, C_HGF:C_HGF + hk], n_ctx)
        hzb = _scan_order(proj[:, :, C_HGF + hk:C_HGF + 2 * hk], n_ctx)
        hv = _scan_order(proj[:, :, C_HGI:C_HGI + hk], n_ctx)
        ohg = _token_order(_hgrn2(hq, hzf, hzb, hv, lower_bounds[layer][None, :], n_ctx), n_ctx)

        wr = jnp.zeros((d, 128), F32).at[:, :N_GROUPS].set(w_router_grp[layer])
        wr = wr.at[:, N_GROUPS:N_GROUPS + N_EXPERTS].set(w_router_exp[layer])
        br = jnp.zeros((1, 128), F32).at[0, :N_GROUPS].set(b_router_grp[layer])
        br = br.at[0, N_GROUPS:N_GROUPS + N_EXPERTS].set(b_router_exp[layer])
        x_mid, h2, route, counts = _merge(
            proj, odn, ohg, xc, modsel, w_br_dn[layer].astype(BF16), w_br_hg[layer].astype(BF16),
            w_out[layer].astype(BF16), dn_norm_g[layer][None, :], hg_norm_g[layer][None, :],
            g_ffn[layer][None, :], wr, br, n_ctx_tiles, tm)

        dest, block_expert, first, n_used, n_blocks = _routing_tables(
            route.reshape(n_tok, 128), counts, n_tok)
        rows = _dispatch(dest, h2.reshape(n_tok, d), n_blocks * MOE_BLOCK, tm)
        y = _experts(block_expert + layer * N_EXPERTS, first, n_used, rows,
                     w_exp_gate.reshape(depth * N_EXPERTS, d, -1),
                     w_exp_up.reshape(depth * N_EXPERTS, d, -1),
                     w_exp_down.reshape(depth * N_EXPERTS, -1, d))
        xc = _combine(dest, y, route, x_mid, modsel, g_final[None, :], n_ctx_tiles, tm, final)

    return xc[:, n_ctx:]
```

```python
import functools

import jax
import jax.numpy as jnp
from jax import lax
from jax.experimental import pallas as pl
from jax.experimental.pallas import tpu as pltpu

F32 = jnp.float32
BF16 = jnp.bfloat16
HI = lax.Precision.HIGHEST

GRID_W = 64
CHUNK = 64
EPS = 1e-6
F_TINY = 1e-30
HEADS = 4
DK = 128
N_GROUPS = 4
EXPERTS_PER_GROUP = 8
N_EXPERTS = 32
TOP_K = 2
MOE_BLOCK = 512
SUB = 16
EXP_CLAMP = 60.0
DN_INTRA_UNROLL = 3
HG_STEP_UNROLL = 4
MERGE_COLS = 256

C_GATES = 0
C_DNQKV = 2048
C_DNZ = 3584
C_HGQ = 4096
C_HGF = 4608
C_HGI = 5632
C_HGG = 6144
N_MAIN = 6656

VMEM_LIMIT = 56 * 1024 * 1024


def _cparams(sem):
    return pltpu.CompilerParams(dimension_semantics=sem, vmem_limit_bytes=VMEM_LIMIT)


def _silu(x):
    return x * jax.nn.sigmoid(x)


def _softplus(x):
    return jnp.maximum(x, 0.0) + jnp.log1p(jnp.exp(-jnp.abs(x)))


def _dot(a, b, prec=None):
    return jnp.dot(a, b, precision=prec, preferred_element_type=F32)


def _dot_nt(a, b, prec=None):
    return lax.dot_general(a, b, (((1,), (1,)), ((), ())), precision=prec,
                           preferred_element_type=F32)


def _dot_tn(a, b, prec=None):
    return lax.dot_general(a, b, (((0,), (0,)), ((), ())), precision=prec,
                           preferred_element_type=F32)


def _bdot(a, b):
    return _dot(a.astype(BF16), b.astype(BF16))


def _bdot_nt(a, b):
    return _dot_nt(a.astype(BF16), b.astype(BF16))


def _bdot_tn(a, b):
    return _dot_tn(a.astype(BF16), b.astype(BF16))


def _mod_kernel(cc_ref, w_ref, b_ref, o_ref):
    s = _silu(cc_ref[...])
    o_ref[0] = _dot(s, w_ref[0], HI) + b_ref[0]


def _modulation(cc, w_ada, b_ada):
    depth, d, n6 = w_ada.shape
    tn = n6 // 4
    return pl.pallas_call(
        _mod_kernel,
        grid=(depth, n6 // tn),
        in_specs=[pl.BlockSpec((16, d), lambda l, j: (0, 0)),
                  pl.BlockSpec((1, d, tn), lambda l, j: (l, 0, j)),
                  pl.BlockSpec((1, 1, tn), lambda l, j: (l, 0, j))],
        out_specs=pl.BlockSpec((1, 16, tn), lambda l, j: (l, 0, j)),
        out_shape=jax.ShapeDtypeStruct((depth, 16, n6), F32),
        compiler_params=_cparams(("parallel", "parallel")),
        name="modulation",
    )(cc, w_ada, b_ada.reshape(depth, 1, n6))


def _proj_kernel(x_ref, mod_ref, g_ref, w_ref, ws_ref, p_ref, s_ref):
    d = x_ref.shape[-1]
    x = x_ref[0]
    m = mod_ref[0]
    h = x * lax.rsqrt(jnp.mean(x * x, axis=-1, keepdims=True) + EPS) * g_ref[...]
    h = h * (1.0 + m[:, d:2 * d]) + m[:, :d]
    p_ref[0] = _dot(h.astype(BF16), w_ref[...])
    s_ref[0] = _dot_nt(ws_ref[...], h, HI)


def _projection(xc, modsel, g, w_main, w_small_t, n_ctx_tiles, tm):
    bsz, n, d = xc.shape
    n_main = w_main.shape[1]
    return pl.pallas_call(
        _proj_kernel,
        grid=(bsz, n // tm),
        in_specs=[pl.BlockSpec((1, tm, d), lambda b, i: (b, i, 0)),
                  pl.BlockSpec((1, 1, modsel.shape[-1]),
                               lambda b, i: (2 * b + (i >= n_ctx_tiles).astype(jnp.int32), 0, 0)),
                  pl.BlockSpec((1, d), lambda b, i: (0, 0)),
                  pl.BlockSpec((d, n_main), lambda b, i: (0, 0), pipeline_mode=pl.Buffered(1)),
                  pl.BlockSpec((32, d), lambda b, i: (0, 0))],
        out_specs=[pl.BlockSpec((1, tm, n_main), lambda b, i: (b, i, 0)),
                   pl.BlockSpec((1, 32, tm), lambda b, i: (b, 0, i))],
        out_shape=[jax.ShapeDtypeStruct((bsz, n, n_main), F32),
                   jax.ShapeDtypeStruct((bsz, 32, n), F32)],
        compiler_params=_cparams(("parallel", "parallel")),
        name="projection",
    )(xc, modsel, g, w_main, w_small_t)


def _tri_masks(fwd):
    ii = lax.broadcasted_iota(jnp.int32, (CHUNK, CHUNK), 0)
    jj = lax.broadcasted_iota(jnp.int32, (CHUNK, CHUNK), 1)
    if fwd:
        return ii >= jj, ii > jj
    return ii <= jj, ii < jj


def _unit_tri_solve(a, rhs, mm):
    ii = lax.broadcasted_iota(jnp.int32, a[0].shape, 0)
    jj = lax.broadcasted_iota(jnp.int32, a[0].shape, 1)
    same = (ii // SUB) == (jj // SUB)
    eye = jnp.where(ii == jj, 1.0, 0.0)
    each = lambda f, *xs: [f(*t) for t in zip(*xs)]
    axpy = lambda x, y: each(lambda xi, yi: xi + mm(xi, yi), x, y)
    dm = each(lambda t: jnp.where(same, t, 0.0), a)
    lm = each(lambda t: jnp.where(same, 0.0, t), a)
    d2 = each(mm, dm, dm)
    d4 = each(mm, d2, d2)
    d8 = each(mm, d4, d4)
    p = axpy(each(lambda t: eye - t, dm), d2)
    p = axpy(p, d4)
    td = axpy(p, d8)
    m = each(mm, td, lm)
    m2 = each(mm, m, m)
    t1 = each(mm, td, rhs)
    t2 = each(lambda x, y: x + mm(y, x), t1, m2)
    return each(lambda x, y: x - mm(y, x), t2, m)


def _chunk_schedule(s, n_ctx_chunks, n_chunks):
    cf = s
    cb = jnp.where(s < n_ctx_chunks, n_ctx_chunks - 1 - s, n_chunks + n_ctx_chunks - 1 - s)
    return pl.multiple_of(cf * CHUNK, CHUNK), pl.multiple_of(cb * CHUNK, CHUNK)


def _seq_cumsum(x, fwd):
    n = x.shape[0]
    pos = lax.broadcasted_iota(jnp.int32, x.shape, 0) % CHUNK
    s = 1
    while s < CHUNK:
        if fwd:
            x = x + jnp.where(pos >= s, pltpu.roll(x, s, 0), 0.0)
        else:
            x = x + jnp.where(pos < CHUNK - s, pltpu.roll(x, n - s, 0), 0.0)
        s *= 2
    return x


def _dn_intra(q, k, v, beta, gc, fwds):
    m = q[0].shape[0]
    ii = lax.broadcasted_iota(jnp.int32, (m, m), 0)
    jj = lax.broadcasted_iota(jnp.int32, (m, m), 1)
    blk = ii // CHUNK
    same = blk == (jj // CHUNK)
    is_fwd = functools.reduce(lambda x, y: x | y,
                              [blk == g for g, f in enumerate(fwds) if f], blk < 0)
    incl = same & ((is_fwd & (ii >= jj)) | (~is_fwd & (ii <= jj)))
    strict = incl & (ii != jj)
    wide = lambda t: jnp.concatenate([t] * (m // DK), axis=1)
    tall = lambda t: jnp.concatenate([t] * (m // DK), axis=0)
    each = lambda f, *xs: [f(*t) for t in zip(*xs)]
    kk = each(_bdot_nt, k, k)
    qk = each(_bdot_nt, q, k)
    dec = each(lambda g: jnp.where(incl, jnp.exp(jnp.where(incl, wide(g) - tall(g.T), 0.0)), 0.0), gc)
    a = each(lambda x, b, d: jnp.where(strict, x * wide(b) * d, 0.0), kk, beta, dec)
    eg = each(jnp.exp, gc)
    rhs = each(lambda vi, ki, b, e: jnp.concatenate([vi * b, ki * (b * e)], axis=1), v, k, beta, eg)
    sol = _unit_tri_solve(a, rhs, _bdot)

    def to_end(g):
        rows = [g[i * CHUNK + (CHUNK - 1 if f else 0)][None, :] for i, f in enumerate(fwds)]
        return jnp.concatenate([jnp.broadcast_to(r, (CHUNK, DK)) for r in rows], axis=0) - g

    n_stack = len(fwds)
    ri = lax.broadcasted_iota(jnp.int32, (n_stack * DK, m), 0) // DK
    ci = lax.broadcasted_iota(jnp.int32, (n_stack * DK, m), 1) // CHUNK
    kdt_bd = each(lambda ki, g: jnp.where(
        ri == ci, jnp.concatenate([(ki * jnp.exp(to_end(g))).T] * n_stack, axis=0), 0.0), k, gc)
    att = each(lambda x, d: jnp.where(incl, x * d, 0.0), qk, dec)
    att_uw = each(_bdot, att, sol)
    kd_uw = each(_bdot, kdt_bd, sol)
    q_eff = each(lambda x, e, aw: x * e - aw[:, DK:], q, eg, att_uw)
    return [(aw[:, :DK], qe, kw[:, :DK], kw[:, DK:]) for aw, qe, kw in zip(att_uw, q_eff, kd_uw)]


def _dn_kernel(q_ref, k_ref, v_ref, cq_ref, ck_ref, cv_ref, sm_ref, alog_ref, dtb_ref,
               o_ref, qs, ks, vs, bfs, bbs, gfs, gbs, qes, cs, ns, *, n_ctx):
    n = q_ref.shape[1]
    row = lax.broadcasted_iota(jnp.int32, (n, DK), 0)
    seg_first = (row == 0) | (row == n_ctx)
    seg_last = (row == n_ctx - 1) | (row == n - 1)

    def conv_act(x_ref, c_ref):
        x = x_ref[0]
        w = c_ref[...]
        xm = jnp.where(seg_first, 0.0, pltpu.roll(x, 1, 0))
        xp = jnp.where(seg_last, 0.0, pltpu.roll(x, n - 1, 0))
        return _silu(xm * w[0:1] + x * w[1:2] + xp * w[2:3])

    def l2n(y):
        return y * lax.rsqrt(jnp.sum(y * y, axis=-1, keepdims=True) + EPS)

    qs[...] = l2n(conv_act(q_ref, cq_ref)) * (DK ** -0.5)
    ks[...] = l2n(conv_act(k_ref, ck_ref))
    vs[...] = conv_act(v_ref, cv_ref)

    sm = sm_ref[0]
    beta = jax.nn.sigmoid(sm)
    g = -jnp.exp(alog_ref[0]) * _softplus(sm + dtb_ref[0])
    pos = lax.broadcasted_iota(jnp.int32, sm.shape, 1) % CHUNK
    pre, suf = g, g
    s = 1
    while s < CHUNK:
        pre = pre + jnp.where(pos >= s, pltpu.roll(pre, s, 1), 0.0)
        suf = suf + jnp.where(pos < CHUNK - s, pltpu.roll(suf, n - s, 1), 0.0)
        s *= 2

    def col(r):
        return jnp.broadcast_to(r, (DK, n)).T

    bfs[...] = col(beta[0:1])
    bbs[...] = col(beta[1:2])
    gfs[...] = col(pre[2:3])
    gbs[...] = col(suf[3:4])

    n_chunks = n // CHUNK
    n_ctx_chunks = n_ctx // CHUNK

    dirs = ((True, bfs, gfs), (False, bbs, gbs))

    def intra(c, carry):
        starts = [pl.multiple_of((c * DN_INTRA_UNROLL + j) * 2 * CHUNK, 2 * CHUNK)
                  for j in range(DN_INTRA_UNROLL)]
        pairs = [pl.ds(r0, 2 * CHUNK) for r0 in starts]
        stack = lambda t: jnp.concatenate([t[:CHUNK], t[:CHUNK], t[CHUNK:], t[CHUNK:]], axis=0)
        both = lambda tf, tb: jnp.concatenate(
            [tf[:CHUNK], tb[:CHUNK], tf[CHUNK:], tb[CHUNK:]], axis=0)
        results = _dn_intra(
            [stack(qs[p, :]) for p in pairs], [stack(ks[p, :]) for p in pairs],
            [stack(vs[p, :]) for p in pairs],
            [both(bfs[p, :], bbs[p, :]) for p in pairs],
            [both(gfs[p, :], gbs[p, :]) for p in pairs], (True, False, True, False))
        for r0, (o0, q_eff, c_all, n_all) in zip(starts, results):
            for g in range(4):
                di = g % 2
                rg = r0 + (g // 2) * CHUNK
                sl = pl.ds(rg, CHUNK)
                sl2 = pl.ds(pl.multiple_of(2 * rg, 2 * CHUNK), DK)
                if di == 0:
                    o_ref[0, sl, :] = o0[g * CHUNK:(g + 1) * CHUNK] + o0[(g + 1) * CHUNK:(g + 2) * CHUNK]
                qes[di, sl, :] = q_eff[g * CHUNK:(g + 1) * CHUNK].astype(BF16)
                cs[di, sl2, :] = c_all[g * DK:(g + 1) * DK]
                ns[di, sl2, :] = n_all[g * DK:(g + 1) * DK].astype(BF16)
        return carry

    lax.fori_loop(0, n_chunks // (2 * DN_INTRA_UNROLL), intra, 0)

    def step(s, states):
        rows = _chunk_schedule(s, n_ctx_chunks, n_chunks)
        sls = [pl.ds(r0, CHUNK) for r0 in rows]
        sl2s = [pl.ds(pl.multiple_of(2 * r0, 2 * CHUNK), DK) for r0 in rows]
        stb = [st.astype(BF16) for st in states]
        corr = [_dot(ns[di, sl2s[di], :], stb[di]) for di in range(2)]
        outs = [_dot(qes[di, sls[di], :], stb[di]) for di in range(2)]
        new_states = []
        for di, (fwd, _, g_scr) in enumerate(dirs):
            g_tot = g_scr[pl.ds(rows[di] + (CHUNK - 1 if fwd else 0), 1), :]
            new_states.append(states[di] * jnp.exp(g_tot) + cs[di, sl2s[di], :] - corr[di])
            o_ref[0, sls[di], :] = o_ref[0, sls[di], :] + outs[di]
        return tuple(new_states)

    zero = jnp.zeros((DK, DK), F32)
    lax.fori_loop(0, n_chunks, step, (zero, zero))


def _deltanet(proj, small, conv_w, alog, dtb, n_ctx):
    bsz, n, _ = proj.shape
    qb = C_DNQKV // DK
    seq = lambda off: pl.BlockSpec((1, n, DK), lambda b, h: (b, 0, off + h))
    cw = lambda off: pl.BlockSpec((3, DK), lambda b, h: (0, off + h))
    return pl.pallas_call(
        functools.partial(_dn_kernel, n_ctx=n_ctx),
        grid=(bsz, HEADS),
        in_specs=[seq(qb), seq(qb + HEADS), seq(qb + 2 * HEADS),
                  cw(0), cw(HEADS), cw(2 * HEADS),
                  pl.BlockSpec((1, 8, n), lambda b, h: (b, h, 0)),
                  pl.BlockSpec((1, 8, 1), lambda b, h: (h, 0, 0)),
                  pl.BlockSpec((1, 8, 1), lambda b, h: (h, 0, 0))],
        out_specs=pl.BlockSpec((1, n, DK), lambda b, h: (b, 0, h)),
        out_shape=jax.ShapeDtypeStruct((bsz, n, HEADS * DK), F32),
        scratch_shapes=[pltpu.VMEM((n, DK), F32) for _ in range(7)] + [
            pltpu.VMEM((2, n, DK), BF16),
            pltpu.VMEM((2, 2 * n, DK), F32),
            pltpu.VMEM((2, 2 * n, DK), BF16)],
        compiler_params=_cparams(("parallel", "parallel")),
        name="deltanet",
    )(proj, proj, proj, conv_w, conv_w, conv_w, small, alog, dtb)


def _hg_intra(q, zf, zb, v, lb):
    fwds = (True, False, True, False)
    n_stack = len(fwds)
    nsub = CHUNK // SUB

    def gate(z):
        f = lb + (1.0 - lb) * jax.nn.sigmoid(z)
        return (1.0 - lb) * jax.nn.sigmoid(-z), jnp.log(jnp.maximum(f, F_TINY))

    kf_f, lf_f = gate(zf)
    kf_b, lf_b = gate(zb)
    both = lambda tf, tb: jnp.concatenate([tf[:CHUNK], tb[:CHUNK], tf[CHUNK:], tb[CHUNK:]], axis=0)
    stack = lambda t: both(t, t)
    q_st = stack(_silu(q))
    v_st = stack(v)
    kf = both(kf_f, kf_b)
    gc = both(_seq_cumsum(lf_f, True), _seq_cumsum(lf_b, False))

    m = n_stack * CHUNK
    ii = lax.broadcasted_iota(jnp.int32, (m, m), 0)
    jj = lax.broadcasted_iota(jnp.int32, (m, m), 1)
    blk = ii // CHUNK
    is_fwd = functools.reduce(lambda x, y: x | y,
                              [blk == g for g, f in enumerate(fwds) if f], blk < 0)
    incl = (blk == (jj // CHUNK)) & ((is_fwd & (ii >= jj)) | (~is_fwd & (ii <= jj)))

    def ref_row(g, i):
        r = g * CHUNK + i * SUB + (0 if fwds[g] else SUB - 1)
        return gc[r:r + 1, :]

    rep = lambda r, k: jnp.broadcast_to(r, (k, DK))
    qe = q_st * jnp.exp(gc - jnp.concatenate(
        [rep(ref_row(g, i), SUB) for g in range(n_stack) for i in range(nsub)], axis=0))
    scores = []
    for i in range(nsub):
        ref = jnp.concatenate([rep(ref_row(g, i), CHUNK) for g in range(n_stack)], axis=0)
        ke = kf * jnp.exp(jnp.minimum(ref - gc, EXP_CLAMP))
        lhs = jnp.concatenate([qe[g * CHUNK + i * SUB:g * CHUNK + (i + 1) * SUB]
                               for g in range(n_stack)], axis=0)
        scores.append(_bdot_nt(lhs, ke))
    att = jnp.concatenate([scores[i][g * SUB:(g + 1) * SUB]
                           for g in range(n_stack) for i in range(nsub)], axis=0)
    o0 = _bdot(jnp.where(incl, att, 0.0), v_st)

    end_rows = [gc[g * CHUNK + (CHUNK - 1 if f else 0)][None, :] for g, f in enumerate(fwds)]
    kd = kf * jnp.exp(jnp.concatenate([rep(r, CHUNK) for r in end_rows], axis=0) - gc)
    ri = lax.broadcasted_iota(jnp.int32, (n_stack * DK, m), 0) // DK
    ci = lax.broadcasted_iota(jnp.int32, (n_stack * DK, m), 1) // CHUNK
    vt_bd = jnp.where(ri == ci, jnp.concatenate([v_st.T] * n_stack, axis=0), 0.0)
    ct = _bdot(vt_bd, kd)
    return o0, q_st * jnp.exp(gc), ct, [jnp.exp(r) for r in end_rows]


def _hg_kernel(q_ref, zf_ref, zb_ref, v_ref, lb_ref, o_ref, oscan, qgs, cts, es, *, n_ctx):
    n = q_ref.shape[1]
    n_lat = n - n_ctx
    col_len = n_lat // GRID_W
    cols_per_pair = 2 * CHUNK // col_len
    lb = lb_ref[...]

    def store(r0, res):
        o0, qg, ct, decay = res
        for g in range(4):
            di = g % 2
            rg = r0 + (g // 2) * CHUNK
            sl = pl.ds(rg, CHUNK)
            if di == 0:
                oscan[sl, :] = o0[g * CHUNK:(g + 1) * CHUNK] + o0[(g + 1) * CHUNK:(g + 2) * CHUNK]
            qgs[di, sl, :] = qg[g * CHUNK:(g + 1) * CHUNK].astype(BF16)
            cts[di, pl.ds(pl.multiple_of(2 * rg, 2 * CHUNK), DK), :] = ct[g * DK:(g + 1) * DK]
            es[di, pl.ds(rg // CHUNK, 1), :] = decay[g]

    def ctx_pair(i, c):
        r0 = pl.multiple_of(i * 2 * CHUNK, 2 * CHUNK)
        sl = pl.ds(r0, 2 * CHUNK)
        store(r0, _hg_intra(q_ref[0, sl, :], zf_ref[0, sl, :], zb_ref[0, sl, :], v_ref[0, sl, :], lb))
        return c

    lax.fori_loop(0, n_ctx // (2 * CHUNK), ctx_pair, 0)

    def lat_pair(i, c):
        def load(ref):
            return jnp.concatenate(
                [ref[0, pl.ds(n_ctx + cols_per_pair * i + j, col_len, stride=GRID_W), :]
                 for j in range(cols_per_pair)], axis=0)

        r0 = pl.multiple_of(n_ctx + i * 2 * CHUNK, 2 * CHUNK)
        store(r0, _hg_intra(load(q_ref), load(zf_ref), load(zb_ref), load(v_ref), lb))
        return c

    lax.fori_loop(0, n_lat // (2 * CHUNK), lat_pair, 0)

    n_chunks = n // CHUNK
    n_ctx_chunks = n_ctx // CHUNK

    def steps(t, states):
        for j in range(HG_STEP_UNROLL):
            rows = _chunk_schedule(t * HG_STEP_UNROLL + j, n_ctx_chunks, n_chunks)
            sls = [pl.ds(r0, CHUNK) for r0 in rows]
            outs = [_dot_nt(qgs[di, sls[di], :], states[di].astype(BF16)) for di in range(2)]
            new_states = []
            for di in range(2):
                decay = es[di, pl.ds(rows[di] // CHUNK, 1), :]
                c = cts[di, pl.ds(pl.multiple_of(2 * rows[di], 2 * CHUNK), DK), :]
                new_states.append(states[di] * decay + c)
                oscan[sls[di], :] = oscan[sls[di], :] + outs[di]
            states = tuple(new_states)
        return states

    zero = jnp.zeros((DK, DK), F32)
    lax.fori_loop(0, n_chunks // HG_STEP_UNROLL, steps, (zero, zero))

    o_ref[0, pl.ds(0, n_ctx), :] = oscan[pl.ds(0, n_ctx), :]

    def grid_row(r, c):
        dst = pl.ds(pl.multiple_of(n_ctx + r * GRID_W, GRID_W), GRID_W)
        o_ref[0, dst, :] = oscan[pl.ds(n_ctx + r, GRID_W, stride=col_len), :]
        return c

    lax.fori_loop(0, col_len, grid_row, 0)


def _hgrn2(proj, lb, n_ctx):
    bsz, n, _ = proj.shape
    seq = lambda off: pl.BlockSpec((1, n, DK), lambda b, h: (b, 0, off // DK + h))
    n_chunks = n // CHUNK
    return pl.pallas_call(
        functools.partial(_hg_kernel, n_ctx=n_ctx),
        grid=(bsz, HEADS),
        in_specs=[seq(C_HGQ), seq(C_HGF), seq(C_HGF + HEADS * DK), seq(C_HGI),
                  pl.BlockSpec((1, DK), lambda b, h: (0, h))],
        out_specs=pl.BlockSpec((1, n, DK), lambda b, h: (b, 0, h)),
        out_shape=jax.ShapeDtypeStruct((bsz, n, HEADS * DK), F32),
        scratch_shapes=[pltpu.VMEM((n, DK), F32),
                        pltpu.VMEM((2, n, DK), BF16),
                        pltpu.VMEM((2, 2 * n, DK), F32),
                        pltpu.VMEM((2, -(-n_chunks // 8) * 8, DK), F32)],
        compiler_params=_cparams(("parallel", "parallel")),
        name="hgrn2",
    )(proj, proj, proj, proj, lb)


def _gated_rmsnorm(o, z, g):
    parts = []
    for h in range(HEADS):
        oh = o[:, h * DK:(h + 1) * DK]
        zh = z[:, h * DK:(h + 1) * DK]
        y = oh * lax.rsqrt(jnp.mean(oh * oh, axis=-1, keepdims=True) + EPS) * g
        parts.append(y * _silu(zh))
    return jnp.concatenate(parts, axis=1)


def _route(lg):
    lane = lax.broadcasted_iota(jnp.int32, lg.shape, 1).astype(F32)
    neg = -1e30
    big = 1e9
    is_grp = lane < N_GROUPS
    gl = jnp.where(is_grp, lg, neg)
    gmax = jnp.max(gl, axis=-1, keepdims=True)
    gsel = jnp.min(jnp.where(gl == gmax, lane, big), axis=-1, keepdims=True)
    gp = 1.0 / jnp.sum(jnp.where(is_grp, jnp.exp(gl - gmax), 0.0), axis=-1, keepdims=True)
    lo = N_GROUPS + EXPERTS_PER_GROUP * gsel
    el = jnp.where((lane >= lo) & (lane < lo + EXPERTS_PER_GROUP), lg, neg)
    m1 = jnp.max(el, axis=-1, keepdims=True)
    i1 = jnp.min(jnp.where(el == m1, lane, big), axis=-1, keepdims=True)
    el2 = jnp.where(lane == i1, neg, el)
    m2 = jnp.max(el2, axis=-1, keepdims=True)
    i2 = jnp.min(jnp.where(el2 == m2, lane, big), axis=-1, keepdims=True)
    r = jnp.exp(m2 - m1)
    g1 = gp / (1.0 + r)
    g2 = g1 * r
    out = jnp.where(lane == 0, i1 - N_GROUPS, 0.0)
    out = jnp.where(lane == 1, i2 - N_GROUPS, out)
    out = jnp.where(lane == 2, g1, out)
    return jnp.where(lane == 3, g2, out)


def _merge_kernel(gate_ref, z_ref, hg_ref, odn_ref, ohg_ref, x_ref, mod_ref, wdn_ref, whg_ref,
                  wout_ref, gdn_ref, ghg_ref, gffn_ref, wrh_ref, wrl_ref, br_ref,
                  xo_ref, h2_ref, rt_ref, cnt_ref, mix_s):
    d = x_ref.shape[-1]
    cols = [slice(j * MERGE_COLS, (j + 1) * MERGE_COLS) for j in range(d // MERGE_COLS)]
    mod = lambda k, cs: mod_ref[0, :, k * d + cs.start:k * d + cs.stop]

    @pl.when((pl.program_id(0) == 0) & (pl.program_id(1) == 0))
    def _():
        cnt_ref[...] = jnp.zeros(cnt_ref.shape, F32)

    a_dn = _gated_rmsnorm(odn_ref[0], z_ref[0], gdn_ref[...]).astype(BF16)
    a_hg = _gated_rmsnorm(ohg_ref[0], hg_ref[0], ghg_ref[...]).astype(BF16)
    for cs in cols:
        br_dn = _dot(a_dn, wdn_ref[:, cs])
        br_hg = _dot(a_hg, whg_ref[:, cs])
        g_dn = jax.nn.sigmoid(gate_ref[0, :, cs])
        g_hg = jax.nn.sigmoid(gate_ref[0, :, d + cs.start:d + cs.stop])
        mix_s[:, cs] = (g_dn * br_dn + g_hg * br_hg).astype(BF16)
    mix = mix_s[...]
    ssq = jnp.zeros((x_ref.shape[1], 1), F32)
    for cs in cols:
        x = x_ref[0, :, cs] + mod(2, cs) * _dot(mix, wout_ref[:, cs])
        xo_ref[0, :, cs] = x
        ssq = ssq + jnp.sum(x * x, axis=-1, keepdims=True)
    scale = lax.rsqrt(ssq * (1.0 / d) + EPS)
    lg = jnp.zeros(rt_ref.shape[1:], F32) + br_ref[...]
    for cs in cols:
        h2 = xo_ref[0, :, cs] * scale * gffn_ref[:, cs] * (1.0 + mod(4, cs)) + mod(3, cs)
        h2_ref[0, :, cs] = h2
        h_hi = h2.astype(BF16)
        h_lo = (h2 - h_hi.astype(F32)).astype(BF16)
        lg = lg + (_dot(h_hi, wrh_ref[cs, :]) + (_dot(h_hi, wrl_ref[cs, :]) + _dot(h_lo, wrh_ref[cs, :])))
    rt = _route(lg)
    tm = rt.shape[0]
    lane = lax.broadcasted_iota(jnp.int32, rt.shape, 1)
    lanef = lane.astype(F32)
    oh1 = jnp.where(lanef == rt[:, 0:1], 1.0, 0.0)
    oh2 = jnp.where(lanef == rt[:, 1:2], 1.0, 0.0)
    both = oh1 + oh2
    ii = lax.broadcasted_iota(jnp.int32, (tm, tm), 0)
    jj = lax.broadcasted_iota(jnp.int32, (tm, tm), 1)
    earlier = jnp.where(ii > jj, 1.0, 0.0).astype(BF16)
    prior = _dot(earlier, both.astype(BF16)) + cnt_ref[0:1, :]
    rank1 = jnp.sum(prior * oh1, axis=-1, keepdims=True)
    rank2 = jnp.sum(prior * oh2, axis=-1, keepdims=True)
    rt = jnp.where(lane == 4, rank1, rt)
    rt_ref[0] = jnp.where(lane == 5, rank2, rt)
    cnt_ref[0:1, :] = cnt_ref[0:1, :] + jnp.sum(both, axis=0, keepdims=True)


def _merge(proj, odn, ohg, xc, modsel, wdn, whg, wout, gdn, ghg, gffn, wr, br, n_ctx_tiles, tm):
    wr_hi = wr.astype(BF16)
    wr_lo = (wr - wr_hi.astype(F32)).astype(BF16)
    bsz, n, d = xc.shape
    hv = HEADS * DK
    tok = lambda w, j: pl.BlockSpec((1, tm, w), lambda b, i: (b, i, j))
    full = lambda a: pl.BlockSpec(a.shape, lambda b, i: (0,) * a.ndim)
    return pl.pallas_call(
        _merge_kernel,
        grid=(bsz, n // tm),
        in_specs=[tok(2 * d, C_GATES // (2 * d)), tok(hv, C_DNZ // hv), tok(hv, C_HGG // hv),
                  tok(hv, 0), tok(hv, 0), tok(d, 0),
                  pl.BlockSpec((1, 1, modsel.shape[-1]),
                               lambda b, i: (2 * b + (i >= n_ctx_tiles).astype(jnp.int32), 0, 0)),
                  full(wdn), full(whg), full(wout), full(gdn), full(ghg), full(gffn),
                  full(wr_hi), full(wr_lo), full(br)],
        out_specs=[tok(d, 0), tok(d, 0), tok(128, 0),
                   pl.BlockSpec((8, 128), lambda b, i: (0, 0))],
        out_shape=[jax.ShapeDtypeStruct((bsz, n, d), F32),
                   jax.ShapeDtypeStruct((bsz, n, d), F32),
                   jax.ShapeDtypeStruct((bsz, n, 128), F32),
                   jax.ShapeDtypeStruct((8, 128), F32)],
        scratch_shapes=[pltpu.VMEM((tm, d), BF16)],
        compiler_params=_cparams(("arbitrary", "arbitrary")),
        name="merge",
    )(proj, proj, proj, odn, ohg, xc, modsel, wdn, whg, wout, gdn, ghg, gffn, wr_hi, wr_lo, br)


def _dispatch_kernel(dest_ref, h_ref, init_ref, rows_ref, sem, *, tile):
    del init_ref
    base = pl.program_id(0) * tile

    def row_copy(r, d):
        return pltpu.make_async_copy(h_ref.at[pl.ds(r, 1)], rows_ref.at[pl.ds(d, 1)], sem)

    def issue(r, c):
        t = base + r
        row_copy(r, dest_ref[TOP_K * t]).start()
        row_copy(r, dest_ref[TOP_K * t + 1]).start()
        return c

    lax.fori_loop(0, tile, issue, 0)

    def drain(r, c):
        row_copy(0, 0).wait()
        row_copy(0, 0).wait()
        return c

    lax.fori_loop(0, tile, drain, 0)


def _dispatch(dest, h2, n_rows, tile):
    n_tok, d = h2.shape
    return pl.pallas_call(
        functools.partial(_dispatch_kernel, tile=tile),
        grid_spec=pltpu.PrefetchScalarGridSpec(
            num_scalar_prefetch=1,
            grid=(n_tok // tile,),
            in_specs=[pl.BlockSpec((tile, d), lambda i, dr: (i, 0)),
                      pl.BlockSpec(memory_space=pl.ANY)],
            out_specs=pl.BlockSpec(memory_space=pl.ANY),
            scratch_shapes=[pltpu.SemaphoreType.DMA(())]),
        out_shape=jax.ShapeDtypeStruct((n_rows, d), F32),
        input_output_aliases={2: 0},
        compiler_params=pltpu.CompilerParams(dimension_semantics=("arbitrary",),
                                             has_side_effects=True),
        name="moe_dispatch",
    )(dest, h2, jnp.zeros((n_rows, d), F32))


def _expert_kernel(be_ref, first_ref, nused_ref, x_ref, wg_ref, wu_ref, wd_ref, o_ref,
                   wg_s, wu_s, wd_s):
    i = pl.program_id(0)

    @pl.when(first_ref[i] == 1)
    def _():
        wg_s[...] = wg_ref[0].astype(BF16)
        wu_s[...] = wu_ref[0].astype(BF16)
        wd_s[...] = wd_ref[0].astype(BF16)

    @pl.when(i < nused_ref[0])
    def _():
        x = x_ref[...].astype(BF16)
        a = _silu(_dot(x, wg_s[...])) * _dot(x, wu_s[...])
        o_ref[...] = _dot(a.astype(BF16), wd_s[...])

    @pl.when(i >= nused_ref[0])
    def _():
        o_ref[...] = jnp.zeros(o_ref.shape, F32)


def _experts(block_expert, first, n_used, rows, wg, wu, wd):
    n_rows, d = rows.shape
    de = wg.shape[-1]
    return pl.pallas_call(
        _expert_kernel,
        grid_spec=pltpu.PrefetchScalarGridSpec(
            num_scalar_prefetch=3,
            grid=(n_rows // MOE_BLOCK,),
            in_specs=[pl.BlockSpec((MOE_BLOCK, d), lambda i, be, f, nu: (jnp.minimum(i, nu[0] - 1), 0)),
                      pl.BlockSpec((1, d, de), lambda i, be, f, nu: (be[i], 0, 0)),
                      pl.BlockSpec((1, d, de), lambda i, be, f, nu: (be[i], 0, 0)),
                      pl.BlockSpec((1, de, d), lambda i, be, f, nu: (be[i], 0, 0))],
            out_specs=pl.BlockSpec((MOE_BLOCK, d), lambda i, be, f, nu: (i, 0)),
            scratch_shapes=[pltpu.VMEM((d, de), BF16), pltpu.VMEM((d, de), BF16),
                            pltpu.VMEM((de, d), BF16)]),
        out_shape=jax.ShapeDtypeStruct((n_rows, d), F32),
        compiler_params=_cparams(("arbitrary",)),
        name="moe_experts",
    )(block_expert, first, n_used, rows, wg, wu, wd)


def _combine_kernel(dest_ref, y_ref, rt_ref, x_ref, mod_ref, gfin_ref, o_ref, buf0, buf1, sem,
                    *, tile, final):
    d = x_ref.shape[-1]
    base = (pl.program_id(0) * pl.num_programs(1) + pl.program_id(1)) * tile

    def row_copy(src, buf, r):
        return pltpu.make_async_copy(y_ref.at[pl.ds(src, 1)], buf.at[pl.ds(r, 1)], sem)

    def issue(r, c):
        t = base + r
        row_copy(dest_ref[TOP_K * t], buf0, r).start()
        row_copy(dest_ref[TOP_K * t + 1], buf1, r).start()
        return c

    lax.fori_loop(0, tile, issue, 0)

    def drain(r, c):
        row_copy(0, buf0, 0).wait()
        row_copy(0, buf1, 0).wait()
        return c

    lax.fori_loop(0, tile, drain, 0)

    rt = rt_ref[0]
    f = rt[:, 2:3] * buf0[...] + rt[:, 3:4] * buf1[...]
    x = x_ref[0] + mod_ref[0][:, 5 * d:6 * d] * f
    if final:
        x = x * lax.rsqrt(jnp.mean(x * x, axis=-1, keepdims=True) + EPS) * gfin_ref[...]
    o_ref[0] = x


def _combine(dest, y, route, xc, modsel, g_final, n_ctx_tiles, tm, final):
    bsz, n, d = xc.shape
    tok = lambda w: pl.BlockSpec((1, tm, w), lambda b, i, dr: (b, i, 0))
    return pl.pallas_call(
        functools.partial(_combine_kernel, tile=tm, final=final),
        grid_spec=pltpu.PrefetchScalarGridSpec(
            num_scalar_prefetch=1,
            grid=(bsz, n // tm),
            in_specs=[pl.BlockSpec(memory_space=pl.ANY), tok(128), tok(d),
                      pl.BlockSpec((1, 1, modsel.shape[-1]),
                                   lambda b, i, dr: (2 * b + (i >= n_ctx_tiles).astype(jnp.int32), 0, 0)),
                      pl.BlockSpec((1, d), lambda b, i, dr: (0, 0))],
            out_specs=tok(d),
            scratch_shapes=[pltpu.VMEM((tm, d), F32), pltpu.VMEM((tm, d), F32),
                            pltpu.SemaphoreType.DMA(())]),
        out_shape=jax.ShapeDtypeStruct((bsz, n, d), F32),
        compiler_params=_cparams(("arbitrary", "arbitrary")),
        name="moe_combine",
    )(dest, y, route, xc, modsel, g_final)


def _routing_tables(route, counts, n_tok):
    e_flat = route[:, :TOP_K].astype(jnp.int32).reshape(n_tok * TOP_K)
    rank = route[:, 4:4 + TOP_K].astype(jnp.int32).reshape(n_tok * TOP_K)
    counts = counts[0, :N_EXPERTS].astype(jnp.int32)
    padded = (counts + MOE_BLOCK - 1) // MOE_BLOCK * MOE_BLOCK
    pad_end = jnp.cumsum(padded)
    pad_start = pad_end - padded
    dest = pad_start[e_flat] + rank
    n_blocks = (n_tok * TOP_K + N_EXPERTS * (MOE_BLOCK - 1) + MOE_BLOCK - 1) // MOE_BLOCK
    block_row = jnp.arange(n_blocks, dtype=jnp.int32) * MOE_BLOCK
    block_expert = jnp.minimum(
        jnp.sum((pad_end[None, :] <= block_row[:, None]).astype(jnp.int32), axis=1),
        N_EXPERTS - 1).astype(jnp.int32)
    first = jnp.concatenate([jnp.ones((1,), jnp.int32),
                             (block_expert[1:] != block_expert[:-1]).astype(jnp.int32)])
    n_used = (pad_end[-1] // MOE_BLOCK).astype(jnp.int32).reshape(1)
    return dest.astype(jnp.int32), block_expert, first, n_used, n_blocks


def kernel(x, c, ctx, c_ctx, w_ada, b_ada, g_mix, g_ffn, g_final, w_in, dn_conv, dn_a_log,
           dn_dt_bias, dn_norm_g, hg_lb_logits, hg_norm_g, w_br_dn, w_br_hg, w_out,
           w_router_grp, b_router_grp, w_router_exp, b_router_exp, w_exp_gate, w_exp_up,
           w_exp_down):
    bsz, n_lat, d = x.shape
    n_ctx = ctx.shape[1]
    depth = w_ada.shape[0]
    n = n_ctx + n_lat
    n_tok = bsz * n
    tm = 256
    n_ctx_tiles = n_ctx // tm
    hk = HEADS * DK

    lb_w = jax.nn.softmax(hg_lb_logits.astype(F32), axis=0)
    lower_bounds = jnp.cumsum(lb_w, axis=0) - lb_w[0]

    cc = jnp.zeros((16, d), F32).at[:bsz].set(c).at[bsz].set(c_ctx)
    mod_all = _modulation(cc, w_ada, b_ada)

    xc = jnp.concatenate([ctx, x], axis=1)

    o_qkv, o_z, o_b, o_a = 0, 3 * hk, 4 * hk, 4 * hk + 2 * HEADS
    o_hq = o_a + 2 * HEADS
    o_hf, o_hi, o_hg, o_gt = o_hq + hk, o_hq + 3 * hk, o_hq + 4 * hk, o_hq + 5 * hk

    for layer in range(depth):
        final = layer == depth - 1
        mod = mod_all[layer]
        modsel = jnp.stack([jnp.broadcast_to(mod[bsz], (bsz, 6 * d)), mod[:bsz]],
                           axis=1).reshape(2 * bsz, 1, 6 * d)
        wl = w_in[layer]
        w_main = jnp.concatenate([wl[:, o_gt:], wl[:, o_qkv:o_b], wl[:, o_hq:o_gt]],
                                 axis=1).astype(BF16)
        wb = wl[:, o_b:o_a].reshape(d, 2, HEADS)
        wa = wl[:, o_a:o_hq].reshape(d, 2, HEADS)
        w_small_t = jnp.concatenate([wb, wa, jnp.zeros((d, 4, HEADS), F32)], axis=1)
        w_small_t = w_small_t.transpose(2, 1, 0).reshape(8 * HEADS, d)
        pad4 = jnp.zeros((HEADS, 4, 1), F32)
        zero2 = jnp.zeros((HEADS, 2, 1), F32)
        alog = jnp.concatenate([zero2, dn_a_log[layer].T[:, :, None], pad4], axis=1)
        dtb = jnp.concatenate([zero2, dn_dt_bias[layer].T[:, :, None], pad4], axis=1)

        proj, small = _projection(xc, modsel, g_mix[layer][None, :], w_main, w_small_t,
                                  n_ctx_tiles, tm)
        odn = _deltanet(proj, small, dn_conv[layer], alog, dtb, n_ctx)

        ohg = _hgrn2(proj, lower_bounds[layer][None, :], n_ctx)

        wr = jnp.zeros((d, 128), F32).at[:, :N_GROUPS].set(w_router_grp[layer])
        wr = wr.at[:, N_GROUPS:N_GROUPS + N_EXPERTS].set(w_router_exp[layer])
        br = jnp.zeros((1, 128), F32).at[0, :N_GROUPS].set(b_router_grp[layer])
        br = br.at[0, N_GROUPS:N_GROUPS + N_EXPERTS].set(b_router_exp[layer])
        x_mid, h2, route, counts = _merge(
            proj, odn, ohg, xc, modsel, w_br_dn[layer].astype(BF16), w_br_hg[layer].astype(BF16),
            w_out[layer].astype(BF16), dn_norm_g[layer][None, :], hg_norm_g[layer][None, :],
            g_ffn[layer][None, :], wr, br, n_ctx_tiles, tm)

        dest, block_expert, first, n_used, n_blocks = _routing_tables(
            route.reshape(n_tok, 128), counts, n_tok)
        rows = _dispatch(dest, h2.reshape(n_tok, d), n_blocks * MOE_BLOCK, tm)
        y = _experts(block_expert + layer * N_EXPERTS, first, n_used, rows,
                     w_exp_gate.reshape(depth * N_EXPERTS, d, -1),
                     w_exp_up.reshape(depth * N_EXPERTS, d, -1),
                     w_exp_down.reshape(depth * N_EXPERTS, -1, d))
        xc = _combine(dest, y, route, x_mid, modsel, g_final[None, :], n_ctx_tiles, tm, final)

    return xc[:, n_ctx:]
```

```python
import functools

import jax
import jax.numpy as jnp
from jax import lax
from jax.experimental import pallas as pl
from jax.experimental.pallas import tpu as pltpu

F32 = jnp.float32
BF16 = jnp.bfloat16
HI = lax.Precision.HIGHEST

GRID_W = 64
CHUNK = 64
EPS = 1e-6
F_TINY = 1e-30
HEADS = 4
DK = 128
N_GROUPS = 4
EXPERTS_PER_GROUP = 8
N_EXPERTS = 32
TOP_K = 2
MOE_BLOCK = 512
SUB = 16
EXP_CLAMP = 60.0
DN_INTRA_UNROLL = 3
HG_STEP_UNROLL = 4
MERGE_COLS = 256
ROW_DMA_UNROLL = 8

C_GATES = 0
C_DNQKV = 2048
C_DNZ = 3584
C_HGQ = 4096
C_HGF = 4608
C_HGI = 5632
C_HGG = 6144
N_MAIN = 6656

VMEM_LIMIT = 56 * 1024 * 1024


def _cparams(sem):
    return pltpu.CompilerParams(dimension_semantics=sem, vmem_limit_bytes=VMEM_LIMIT)


def _silu(x):
    return x * jax.nn.sigmoid(x)


def _softplus(x):
    return jnp.maximum(x, 0.0) + jnp.log1p(jnp.exp(-jnp.abs(x)))


def _dot(a, b, prec=None):
    return jnp.dot(a, b, precision=prec, preferred_element_type=F32)


def _dot_nt(a, b, prec=None):
    return lax.dot_general(a, b, (((1,), (1,)), ((), ())), precision=prec,
                           preferred_element_type=F32)


def _dot_tn(a, b, prec=None):
    return lax.dot_general(a, b, (((0,), (0,)), ((), ())), precision=prec,
                           preferred_element_type=F32)


def _bdot(a, b):
    return _dot(a.astype(BF16), b.astype(BF16))


def _bdot_nt(a, b):
    return _dot_nt(a.astype(BF16), b.astype(BF16))


def _bdot_tn(a, b):
    return _dot_tn(a.astype(BF16), b.astype(BF16))


def _mod_kernel(cc_ref, w_ref, b_ref, o_ref):
    s = _silu(cc_ref[...])
    o_ref[0] = _dot(s, w_ref[0], HI) + b_ref[0]


def _modulation(cc, w_ada, b_ada):
    depth, d, n6 = w_ada.shape
    tn = n6 // 4
    return pl.pallas_call(
        _mod_kernel,
        grid=(depth, n6 // tn),
        in_specs=[pl.BlockSpec((16, d), lambda l, j: (0, 0)),
                  pl.BlockSpec((1, d, tn), lambda l, j: (l, 0, j)),
                  pl.BlockSpec((1, 1, tn), lambda l, j: (l, 0, j))],
        out_specs=pl.BlockSpec((1, 16, tn), lambda l, j: (l, 0, j)),
        out_shape=jax.ShapeDtypeStruct((depth, 16, n6), F32),
        compiler_params=_cparams(("parallel", "parallel")),
        name="modulation",
    )(cc, w_ada, b_ada.reshape(depth, 1, n6))


def _proj_kernel(x_ref, mod_ref, g_ref, w_ref, ws_ref, p_ref, s_ref):
    d = x_ref.shape[-1]
    x = x_ref[0]
    m = mod_ref[0]
    h = x * lax.rsqrt(jnp.mean(x * x, axis=-1, keepdims=True) + EPS) * g_ref[...]
    h = h * (1.0 + m[:, d:2 * d]) + m[:, :d]
    p_ref[0] = _dot(h.astype(BF16), w_ref[...])
    s_ref[0] = _dot_nt(ws_ref[...], h, HI)


def _combine_proj_kernel(dest_ref, y_ref, rt_ref, x_ref, modp_ref, mod_ref, g_ref, w_ref, ws_ref,
                         xo_ref, p_ref, s_ref, bufs, sems, *, tile):
    d = x_ref.shape[-1]
    steps_per_batch = pl.num_programs(1)
    step = pl.program_id(0) * steps_per_batch + pl.program_id(1)
    last = pl.num_programs(0) * steps_per_batch - 1
    slot = step % 2

    def row_copy(src, sl, k, r):
        return pltpu.make_async_copy(y_ref.at[pl.ds(src, 1)], bufs.at[sl, k, pl.ds(r, 1)],
                                     sems.at[sl])

    def issue_row(base, sl, r):
        t = base + r
        row_copy(dest_ref[TOP_K * t], sl, 0, r).start()
        row_copy(dest_ref[TOP_K * t + 1], sl, 1, r).start()

    def drain(sl):
        def wait_row(r, c):
            row_copy(0, sl, 0, 0).wait()
            row_copy(0, sl, 1, 0).wait()
            return c

        lax.fori_loop(0, tile, wait_row, 0, unroll=ROW_DMA_UNROLL)

    @pl.when(step == 0)
    def _():
        def first(r, c):
            issue_row(0, 0, r)
            return c

        lax.fori_loop(0, tile, first, 0, unroll=ROW_DMA_UNROLL)

    drain(slot)
    rt = rt_ref[0]
    f = rt[:, 2:3] * bufs[slot, 0] + rt[:, 3:4] * bufs[slot, 1]
    x = x_ref[0] + modp_ref[0][:, 5 * d:6 * d] * f
    xo_ref[0] = x

    next_base = jnp.minimum(step + 1, last) * tile
    for r in range(tile):
        issue_row(next_base, 1 - slot, r)

    m = mod_ref[0]
    h = x * lax.rsqrt(jnp.mean(x * x, axis=-1, keepdims=True) + EPS) * g_ref[...]
    h = h * (1.0 + m[:, d:2 * d]) + m[:, :d]
    p_ref[0] = _dot(h.astype(BF16), w_ref[...])
    s_ref[0] = _dot_nt(ws_ref[...], h, HI)

    @pl.when(step == last)
    def _():
        drain(1 - slot)


def _combine_projection(dest, y, route, x_mid, modsel_prev, modsel, g, w_main, w_small_t,
                        n_ctx_tiles, tm):
    bsz, n, d = x_mid.shape
    n_main = w_main.shape[1]
    tok = lambda w: pl.BlockSpec((1, tm, w), lambda b, i, dr: (b, i, 0))
    mod_spec = pl.BlockSpec(
        (1, 1, modsel.shape[-1]),
        lambda b, i, dr: (2 * b + (i >= n_ctx_tiles).astype(jnp.int32), 0, 0))
    return pl.pallas_call(
        functools.partial(_combine_proj_kernel, tile=tm),
        grid_spec=pltpu.PrefetchScalarGridSpec(
            num_scalar_prefetch=1,
            grid=(bsz, n // tm),
            in_specs=[pl.BlockSpec(memory_space=pl.ANY), tok(128), tok(d), mod_spec, mod_spec,
                      pl.BlockSpec((1, d), lambda b, i, dr: (0, 0)),
                      pl.BlockSpec((d, n_main), lambda b, i, dr: (0, 0),
                                   pipeline_mode=pl.Buffered(1)),
                      pl.BlockSpec((32, d), lambda b, i, dr: (0, 0))],
            out_specs=[tok(d), tok(n_main),
                       pl.BlockSpec((1, 32, tm), lambda b, i, dr: (b, 0, i))],
            scratch_shapes=[pltpu.VMEM((2, TOP_K, tm, d), F32), pltpu.SemaphoreType.DMA((2,))]),
        out_shape=[jax.ShapeDtypeStruct((bsz, n, d), F32),
                   jax.ShapeDtypeStruct((bsz, n, n_main), F32),
                   jax.ShapeDtypeStruct((bsz, 32, n), F32)],
        compiler_params=_cparams(("arbitrary", "arbitrary")),
        name="combine_projection",
    )(dest, y, route, x_mid, modsel_prev, modsel, g, w_main, w_small_t)


def _projection(xc, modsel, g, w_main, w_small_t, n_ctx_tiles, tm):
    bsz, n, d = xc.shape
    n_main = w_main.shape[1]
    return pl.pallas_call(
        _proj_kernel,
        grid=(bsz, n // tm),
        in_specs=[pl.BlockSpec((1, tm, d), lambda b, i: (b, i, 0)),
                  pl.BlockSpec((1, 1, modsel.shape[-1]),
                               lambda b, i: (2 * b + (i >= n_ctx_tiles).astype(jnp.int32), 0, 0)),
                  pl.BlockSpec((1, d), lambda b, i: (0, 0)),
                  pl.BlockSpec((d, n_main), lambda b, i: (0, 0), pipeline_mode=pl.Buffered(1)),
                  pl.BlockSpec((32, d), lambda b, i: (0, 0))],
        out_specs=[pl.BlockSpec((1, tm, n_main), lambda b, i: (b, i, 0)),
                   pl.BlockSpec((1, 32, tm), lambda b, i: (b, 0, i))],
        out_shape=[jax.ShapeDtypeStruct((bsz, n, n_main), F32),
                   jax.ShapeDtypeStruct((bsz, 32, n), F32)],
        compiler_params=_cparams(("parallel", "parallel")),
        name="projection",
    )(xc, modsel, g, w_main, w_small_t)


def _tri_masks(fwd):
    ii = lax.broadcasted_iota(jnp.int32, (CHUNK, CHUNK), 0)
    jj = lax.broadcasted_iota(jnp.int32, (CHUNK, CHUNK), 1)
    if fwd:
        return ii >= jj, ii > jj
    return ii <= jj, ii < jj


def _unit_tri_solve(a, rhs, mm):
    ii = lax.broadcasted_iota(jnp.int32, a[0].shape, 0)
    jj = lax.broadcasted_iota(jnp.int32, a[0].shape, 1)
    same = (ii // SUB) == (jj // SUB)
    eye = jnp.where(ii == jj, 1.0, 0.0)
    each = lambda f, *xs: [f(*t) for t in zip(*xs)]
    axpy = lambda x, y: each(lambda xi, yi: xi + mm(xi, yi), x, y)
    dm = each(lambda t: jnp.where(same, t, 0.0), a)
    lm = each(lambda t: jnp.where(same, 0.0, t), a)
    d2 = each(mm, dm, dm)
    d4 = each(mm, d2, d2)
    d8 = each(mm, d4, d4)
    p = axpy(each(lambda t: eye - t, dm), d2)
    p = axpy(p, d4)
    td = axpy(p, d8)
    m = each(mm, td, lm)
    m2 = each(mm, m, m)
    t1 = each(mm, td, rhs)
    t2 = each(lambda x, y: x + mm(y, x), t1, m2)
    return each(lambda x, y: x - mm(y, x), t2, m)


def _chunk_schedule(s, n_ctx_chunks, n_chunks):
    cf = s
    cb = jnp.where(s < n_ctx_chunks, n_ctx_chunks - 1 - s, n_chunks + n_ctx_chunks - 1 - s)
    return pl.multiple_of(cf * CHUNK, CHUNK), pl.multiple_of(cb * CHUNK, CHUNK)


def _seq_cumsum(x, fwd):
    n = x.shape[0]
    pos = lax.broadcasted_iota(jnp.int32, x.shape, 0) % CHUNK
    s = 1
    while s < CHUNK:
        if fwd:
            x = x + jnp.where(pos >= s, pltpu.roll(x, s, 0), 0.0)
        else:
            x = x + jnp.where(pos < CHUNK - s, pltpu.roll(x, n - s, 0), 0.0)
        s *= 2
    return x


def _dn_intra(q, k, v, beta, gc, fwds):
    m = q[0].shape[0]
    ii = lax.broadcasted_iota(jnp.int32, (m, m), 0)
    jj = lax.broadcasted_iota(jnp.int32, (m, m), 1)
    blk = ii // CHUNK
    same = blk == (jj // CHUNK)
    is_fwd = functools.reduce(lambda x, y: x | y,
                              [blk == g for g, f in enumerate(fwds) if f], blk < 0)
    incl = same & ((is_fwd & (ii >= jj)) | (~is_fwd & (ii <= jj)))
    strict = incl & (ii != jj)
    wide = lambda t: jnp.concatenate([t] * (m // DK), axis=1)
    tall = lambda t: jnp.concatenate([t] * (m // DK), axis=0)
    each = lambda f, *xs: [f(*t) for t in zip(*xs)]
    kk = each(_bdot_nt, k, k)
    qk = each(_bdot_nt, q, k)
    dec = each(lambda g: jnp.where(incl, jnp.exp(jnp.where(incl, wide(g) - tall(g.T), 0.0)), 0.0), gc)
    a = each(lambda x, b, d: jnp.where(strict, x * wide(b) * d, 0.0), kk, beta, dec)
    eg = each(jnp.exp, gc)
    rhs = each(lambda vi, ki, b, e: jnp.concatenate([vi * b, ki * (b * e)], axis=1), v, k, beta, eg)
    sol = _unit_tri_solve(a, rhs, _bdot)

    def to_end(g):
        rows = [g[i * CHUNK + (CHUNK - 1 if f else 0)][None, :] for i, f in enumerate(fwds)]
        return jnp.concatenate([jnp.broadcast_to(r, (CHUNK, DK)) for r in rows], axis=0) - g

    n_stack = len(fwds)
    ri = lax.broadcasted_iota(jnp.int32, (n_stack * DK, m), 0) // DK
    ci = lax.broadcasted_iota(jnp.int32, (n_stack * DK, m), 1) // CHUNK
    kdt_bd = each(lambda ki, g: jnp.where(
        ri == ci, jnp.concatenate([(ki * jnp.exp(to_end(g))).T] * n_stack, axis=0), 0.0), k, gc)
    att = each(lambda x, d: jnp.where(incl, x * d, 0.0), qk, dec)
    att_uw = each(_bdot, att, sol)
    kd_uw = each(_bdot, kdt_bd, sol)
    q_eff = each(lambda x, e, aw: x * e - aw[:, DK:], q, eg, att_uw)
    return [(aw[:, :DK], qe, kw[:, :DK], kw[:, DK:]) for aw, qe, kw in zip(att_uw, q_eff, kd_uw)]


def _dn_kernel(q_ref, k_ref, v_ref, cq_ref, ck_ref, cv_ref, sm_ref, alog_ref, dtb_ref,
               o_ref, qs, ks, vs, bfs, bbs, gfs, gbs, qes, cs, ns, *, n_ctx):
    n = q_ref.shape[1]
    row = lax.broadcasted_iota(jnp.int32, (n, DK), 0)
    seg_first = (row == 0) | (row == n_ctx)
    seg_last = (row == n_ctx - 1) | (row == n - 1)

    def conv_act(x_ref, c_ref):
        x = x_ref[0]
        w = c_ref[...]
        xm = jnp.where(seg_first, 0.0, pltpu.roll(x, 1, 0))
        xp = jnp.where(seg_last, 0.0, pltpu.roll(x, n - 1, 0))
        return _silu(xm * w[0:1] + x * w[1:2] + xp * w[2:3])

    def l2n(y):
        return y * lax.rsqrt(jnp.sum(y * y, axis=-1, keepdims=True) + EPS)

    qs[...] = l2n(conv_act(q_ref, cq_ref)) * (DK ** -0.5)
    ks[...] = l2n(conv_act(k_ref, ck_ref))
    vs[...] = conv_act(v_ref, cv_ref)

    sm = sm_ref[0]
    beta = jax.nn.sigmoid(sm)
    g = -jnp.exp(alog_ref[0]) * _softplus(sm + dtb_ref[0])
    pos = lax.broadcasted_iota(jnp.int32, sm.shape, 1) % CHUNK
    pre, suf = g, g
    s = 1
    while s < CHUNK:
        pre = pre + jnp.where(pos >= s, pltpu.roll(pre, s, 1), 0.0)
        suf = suf + jnp.where(pos < CHUNK - s, pltpu.roll(suf, n - s, 1), 0.0)
        s *= 2

    def col(r):
        return jnp.broadcast_to(r, (DK, n)).T

    bfs[...] = col(beta[0:1])
    bbs[...] = col(beta[1:2])
    gfs[...] = col(pre[2:3])
    gbs[...] = col(suf[3:4])

    n_chunks = n // CHUNK
    n_ctx_chunks = n_ctx // CHUNK

    dirs = ((True, bfs, gfs), (False, bbs, gbs))

    def intra(c, carry):
        starts = [pl.multiple_of((c * DN_INTRA_UNROLL + j) * 2 * CHUNK, 2 * CHUNK)
                  for j in range(DN_INTRA_UNROLL)]
        pairs = [pl.ds(r0, 2 * CHUNK) for r0 in starts]
        stack = lambda t: jnp.concatenate([t[:CHUNK], t[:CHUNK], t[CHUNK:], t[CHUNK:]], axis=0)
        both = lambda tf, tb: jnp.concatenate(
            [tf[:CHUNK], tb[:CHUNK], tf[CHUNK:], tb[CHUNK:]], axis=0)
        results = _dn_intra(
            [stack(qs[p, :]) for p in pairs], [stack(ks[p, :]) for p in pairs],
            [stack(vs[p, :]) for p in pairs],
            [both(bfs[p, :], bbs[p, :]) for p in pairs],
            [both(gfs[p, :], gbs[p, :]) for p in pairs], (True, False, True, False))
        for r0, (o0, q_eff, c_all, n_all) in zip(starts, results):
            for g in range(4):
                di = g % 2
                rg = r0 + (g // 2) * CHUNK
                sl = pl.ds(rg, CHUNK)
                sl2 = pl.ds(pl.multiple_of(2 * rg, 2 * CHUNK), DK)
                if di == 0:
                    o_ref[0, sl, :] = o0[g * CHUNK:(g + 1) * CHUNK] + o0[(g + 1) * CHUNK:(g + 2) * CHUNK]
                qes[di, sl, :] = q_eff[g * CHUNK:(g + 1) * CHUNK].astype(BF16)
                cs[di, sl2, :] = c_all[g * DK:(g + 1) * DK]
                ns[di, sl2, :] = n_all[g * DK:(g + 1) * DK].astype(BF16)
        return carry

    lax.fori_loop(0, n_chunks // (2 * DN_INTRA_UNROLL), intra, 0)

    def step(s, states):
        rows = _chunk_schedule(s, n_ctx_chunks, n_chunks)
        sls = [pl.ds(r0, CHUNK) for r0 in rows]
        sl2s = [pl.ds(pl.multiple_of(2 * r0, 2 * CHUNK), DK) for r0 in rows]
        stb = [st.astype(BF16) for st in states]
        corr = [_dot(ns[di, sl2s[di], :], stb[di]) for di in range(2)]
        outs = [_dot(qes[di, sls[di], :], stb[di]) for di in range(2)]
        new_states = []
        for di, (fwd, _, g_scr) in enumerate(dirs):
            g_tot = g_scr[pl.ds(rows[di] + (CHUNK - 1 if fwd else 0), 1), :]
            new_states.append(states[di] * jnp.exp(g_tot) + cs[di, sl2s[di], :] - corr[di])
            o_ref[0, sls[di], :] = o_ref[0, sls[di], :] + outs[di]
        return tuple(new_states)

    zero = jnp.zeros((DK, DK), F32)
    lax.fori_loop(0, n_chunks, step, (zero, zero))


def _deltanet(proj, small, conv_w, alog, dtb, n_ctx):
    bsz, n, _ = proj.shape
    qb = C_DNQKV // DK
    seq = lambda off: pl.BlockSpec((1, n, DK), lambda b, h: (b, 0, off + h))
    cw = lambda off: pl.BlockSpec((3, DK), lambda b, h: (0, off + h))
    return pl.pallas_call(
        functools.partial(_dn_kernel, n_ctx=n_ctx),
        grid=(bsz, HEADS),
        in_specs=[seq(qb), seq(qb + HEADS), seq(qb + 2 * HEADS),
                  cw(0), cw(HEADS), cw(2 * HEADS),
                  pl.BlockSpec((1, 8, n), lambda b, h: (b, h, 0)),
                  pl.BlockSpec((1, 8, 1), lambda b, h: (h, 0, 0)),
                  pl.BlockSpec((1, 8, 1), lambda b, h: (h, 0, 0))],
        out_specs=pl.BlockSpec((1, n, DK), lambda b, h: (b, 0, h)),
        out_shape=jax.ShapeDtypeStruct((bsz, n, HEADS * DK), F32),
        scratch_shapes=[pltpu.VMEM((n, DK), F32) for _ in range(7)] + [
            pltpu.VMEM((2, n, DK), BF16),
            pltpu.VMEM((2, 2 * n, DK), F32),
            pltpu.VMEM((2, 2 * n, DK), BF16)],
        compiler_params=_cparams(("parallel", "parallel")),
        name="deltanet",
    )(proj, proj, proj, conv_w, conv_w, conv_w, small, alog, dtb)


def _hg_intra(q, zf, zb, v, lb):
    fwds = (True, False, True, False)
    n_stack = len(fwds)
    nsub = CHUNK // SUB

    def gate(z):
        f = lb + (1.0 - lb) * jax.nn.sigmoid(z)
        return (1.0 - lb) * jax.nn.sigmoid(-z), jnp.log(jnp.maximum(f, F_TINY))

    kf_f, lf_f = gate(zf)
    kf_b, lf_b = gate(zb)
    both = lambda tf, tb: jnp.concatenate([tf[:CHUNK], tb[:CHUNK], tf[CHUNK:], tb[CHUNK:]], axis=0)
    stack = lambda t: both(t, t)
    q_st = stack(_silu(q))
    v_st = stack(v)
    kf = both(kf_f, kf_b)
    gc = both(_seq_cumsum(lf_f, True), _seq_cumsum(lf_b, False))

    m = n_stack * CHUNK
    ii = lax.broadcasted_iota(jnp.int32, (m, m), 0)
    jj = lax.broadcasted_iota(jnp.int32, (m, m), 1)
    blk = ii // CHUNK
    is_fwd = functools.reduce(lambda x, y: x | y,
                              [blk == g for g, f in enumerate(fwds) if f], blk < 0)
    incl = (blk == (jj // CHUNK)) & ((is_fwd & (ii >= jj)) | (~is_fwd & (ii <= jj)))

    def ref_row(g, i):
        r = g * CHUNK + i * SUB + (0 if fwds[g] else SUB - 1)
        return gc[r:r + 1, :]

    rep = lambda r, k: jnp.broadcast_to(r, (k, DK))
    qe = q_st * jnp.exp(gc - jnp.concatenate(
        [rep(ref_row(g, i), SUB) for g in range(n_stack) for i in range(nsub)], axis=0))
    scores = []
    for i in range(nsub):
        ref = jnp.concatenate([rep(ref_row(g, i), CHUNK) for g in range(n_stack)], axis=0)
        ke = kf * jnp.exp(jnp.minimum(ref - gc, EXP_CLAMP))
        lhs = jnp.concatenate([qe[g * CHUNK + i * SUB:g * CHUNK + (i + 1) * SUB]
                               for g in range(n_stack)], axis=0)
        scores.append(_bdot_nt(lhs, ke))
    att = jnp.concatenate([scores[i][g * SUB:(g + 1) * SUB]
                           for g in range(n_stack) for i in range(nsub)], axis=0)
    o0 = _bdot(jnp.where(incl, att, 0.0), v_st)

    end_rows = [gc[g * CHUNK + (CHUNK - 1 if f else 0)][None, :] for g, f in enumerate(fwds)]
    kd = kf * jnp.exp(jnp.concatenate([rep(r, CHUNK) for r in end_rows], axis=0) - gc)
    ri = lax.broadcasted_iota(jnp.int32, (n_stack * DK, m), 0) // DK
    ci = lax.broadcasted_iota(jnp.int32, (n_stack * DK, m), 1) // CHUNK
    vt_bd = jnp.where(ri == ci, jnp.concatenate([v_st.T] * n_stack, axis=0), 0.0)
    ct = _bdot(vt_bd, kd)
    return o0, q_st * jnp.exp(gc), ct, [jnp.exp(r) for r in end_rows]


def _hg_kernel(q_ref, zf_ref, zb_ref, v_ref, lb_ref, o_ref, oscan, qgs, cts, es, *, n_ctx):
    n = q_ref.shape[1]
    n_lat = n - n_ctx
    col_len = n_lat // GRID_W
    cols_per_pair = 2 * CHUNK // col_len
    lb = lb_ref[...]

    def store(r0, res):
        o0, qg, ct, decay = res
        for g in range(4):
            di = g % 2
            rg = r0 + (g // 2) * CHUNK
            sl = pl.ds(rg, CHUNK)
            if di == 0:
                oscan[sl, :] = o0[g * CHUNK:(g + 1) * CHUNK] + o0[(g + 1) * CHUNK:(g + 2) * CHUNK]
            qgs[di, sl, :] = qg[g * CHUNK:(g + 1) * CHUNK].astype(BF16)
            cts[di, pl.ds(pl.multiple_of(2 * rg, 2 * CHUNK), DK), :] = ct[g * DK:(g + 1) * DK]
            es[di, pl.ds(rg // CHUNK, 1), :] = decay[g]

    def ctx_pair(i, c):
        r0 = pl.multiple_of(i * 2 * CHUNK, 2 * CHUNK)
        sl = pl.ds(r0, 2 * CHUNK)
        store(r0, _hg_intra(q_ref[0, sl, :], zf_ref[0, sl, :], zb_ref[0, sl, :], v_ref[0, sl, :], lb))
        return c

    lax.fori_loop(0, n_ctx // (2 * CHUNK), ctx_pair, 0)

    def lat_pair(i, c):
        def load(ref):
            return jnp.concatenate(
                [ref[0, pl.ds(n_ctx + cols_per_pair * i + j, col_len, stride=GRID_W), :]
                 for j in range(cols_per_pair)], axis=0)

        r0 = pl.multiple_of(n_ctx + i * 2 * CHUNK, 2 * CHUNK)
        store(r0, _hg_intra(load(q_ref), load(zf_ref), load(zb_ref), load(v_ref), lb))
        return c

    lax.fori_loop(0, n_lat // (2 * CHUNK), lat_pair, 0)

    n_chunks = n // CHUNK
    n_ctx_chunks = n_ctx // CHUNK

    def steps(t, states):
        for j in range(HG_STEP_UNROLL):
            rows = _chunk_schedule(t * HG_STEP_UNROLL + j, n_ctx_chunks, n_chunks)
            sls = [pl.ds(r0, CHUNK) for r0 in rows]
            outs = [_dot_nt(qgs[di, sls[di], :], states[di].astype(BF16)) for di in range(2)]
            new_states = []
            for di in range(2):
                decay = es[di, pl.ds(rows[di] // CHUNK, 1), :]
                c = cts[di, pl.ds(pl.multiple_of(2 * rows[di], 2 * CHUNK), DK), :]
                new_states.append(states[di] * decay + c)
                oscan[sls[di], :] = oscan[sls[di], :] + outs[di]
            states = tuple(new_states)
        return states

    zero = jnp.zeros((DK, DK), F32)
    lax.fori_loop(0, n_chunks // HG_STEP_UNROLL, steps, (zero, zero))

    o_ref[0, pl.ds(0, n_ctx), :] = oscan[pl.ds(0, n_ctx), :]

    def grid_row(r, c):
        dst = pl.ds(pl.multiple_of(n_ctx + r * GRID_W, GRID_W), GRID_W)
        o_ref[0, dst, :] = oscan[pl.ds(n_ctx + r, GRID_W, stride=col_len), :]
        return c

    lax.fori_loop(0, col_len, grid_row, 0)


def _hgrn2(proj, lb, n_ctx):
    bsz, n, _ = proj.shape
    seq = lambda off: pl.BlockSpec((1, n, DK), lambda b, h: (b, 0, off // DK + h))
    n_chunks = n // CHUNK
    return pl.pallas_call(
        functools.partial(_hg_kernel, n_ctx=n_ctx),
        grid=(bsz, HEADS),
        in_specs=[seq(C_HGQ), seq(C_HGF), seq(C_HGF + HEADS * DK), seq(C_HGI),
                  pl.BlockSpec((1, DK), lambda b, h: (0, h))],
        out_specs=pl.BlockSpec((1, n, DK), lambda b, h: (b, 0, h)),
        out_shape=jax.ShapeDtypeStruct((bsz, n, HEADS * DK), F32),
        scratch_shapes=[pltpu.VMEM((n, DK), F32),
                        pltpu.VMEM((2, n, DK), BF16),
                        pltpu.VMEM((2, 2 * n, DK), F32),
                        pltpu.VMEM((2, -(-n_chunks // 8) * 8, DK), F32)],
        compiler_params=_cparams(("parallel", "parallel")),
        name="hgrn2",
    )(proj, proj, proj, proj, lb)


def _gated_rmsnorm(o, z, g):
    parts = []
    for h in range(HEADS):
        oh = o[:, h * DK:(h + 1) * DK]
        zh = z[:, h * DK:(h + 1) * DK]
        y = oh * lax.rsqrt(jnp.mean(oh * oh, axis=-1, keepdims=True) + EPS) * g
        parts.append(y * _silu(zh))
    return jnp.concatenate(parts, axis=1)


def _route(lg):
    lane = lax.broadcasted_iota(jnp.int32, lg.shape, 1).astype(F32)
    neg = -1e30
    big = 1e9
    is_grp = lane < N_GROUPS
    gl = jnp.where(is_grp, lg, neg)
    gmax = jnp.max(gl, axis=-1, keepdims=True)
    gsel = jnp.min(jnp.where(gl == gmax, lane, big), axis=-1, keepdims=True)
    gp = 1.0 / jnp.sum(jnp.where(is_grp, jnp.exp(gl - gmax), 0.0), axis=-1, keepdims=True)
    lo = N_GROUPS + EXPERTS_PER_GROUP * gsel
    el = jnp.where((lane >= lo) & (lane < lo + EXPERTS_PER_GROUP), lg, neg)
    m1 = jnp.max(el, axis=-1, keepdims=True)
    i1 = jnp.min(jnp.where(el == m1, lane, big), axis=-1, keepdims=True)
    el2 = jnp.where(lane == i1, neg, el)
    m2 = jnp.max(el2, axis=-1, keepdims=True)
    i2 = jnp.min(jnp.where(el2 == m2, lane, big), axis=-1, keepdims=True)
    r = jnp.exp(m2 - m1)
    g1 = gp / (1.0 + r)
    g2 = g1 * r
    out = jnp.where(lane == 0, i1 - N_GROUPS, 0.0)
    out = jnp.where(lane == 1, i2 - N_GROUPS, out)
    out = jnp.where(lane == 2, g1, out)
    return jnp.where(lane == 3, g2, out)


def _merge_kernel(gate_ref, z_ref, hg_ref, odn_ref, ohg_ref, x_ref, mod_ref, wdn_ref, whg_ref,
                  wout_ref, gdn_ref, ghg_ref, gffn_ref, wrh_ref, wrl_ref, br_ref,
                  xo_ref, h2_ref, rt_ref, cnt_ref, mix_s):
    d = x_ref.shape[-1]
    cols = [slice(j * MERGE_COLS, (j + 1) * MERGE_COLS) for j in range(d // MERGE_COLS)]
    mod = lambda k, cs: mod_ref[0, :, k * d + cs.start:k * d + cs.stop]

    @pl.when((pl.program_id(0) == 0) & (pl.program_id(1) == 0))
    def _():
        cnt_ref[...] = jnp.zeros(cnt_ref.shape, F32)

    a_dn = _gated_rmsnorm(odn_ref[0], z_ref[0], gdn_ref[...]).astype(BF16)
    a_hg = _gated_rmsnorm(ohg_ref[0], hg_ref[0], ghg_ref[...]).astype(BF16)
    for cs in cols:
        br_dn = _dot(a_dn, wdn_ref[:, cs])
        br_hg = _dot(a_hg, whg_ref[:, cs])
        g_dn = jax.nn.sigmoid(gate_ref[0, :, cs])
        g_hg = jax.nn.sigmoid(gate_ref[0, :, d + cs.start:d + cs.stop])
        mix_s[:, cs] = (g_dn * br_dn + g_hg * br_hg).astype(BF16)
    mix = mix_s[...]
    ssq = jnp.zeros((x_ref.shape[1], 1), F32)
    for cs in cols:
        x = x_ref[0, :, cs] + mod(2, cs) * _dot(mix, wout_ref[:, cs])
        xo_ref[0, :, cs] = x
        ssq = ssq + jnp.sum(x * x, axis=-1, keepdims=True)
    scale = lax.rsqrt(ssq * (1.0 / d) + EPS)
    lg = jnp.zeros(rt_ref.shape[1:], F32) + br_ref[...]
    for cs in cols:
        h2 = xo_ref[0, :, cs] * scale * gffn_ref[:, cs] * (1.0 + mod(4, cs)) + mod(3, cs)
        h2_ref[0, :, cs] = h2
        h_hi = h2.astype(BF16)
        h_lo = (h2 - h_hi.astype(F32)).astype(BF16)
        lg = lg + (_dot(h_hi, wrh_ref[cs, :]) + (_dot(h_hi, wrl_ref[cs, :]) + _dot(h_lo, wrh_ref[cs, :])))
    rt = _route(lg)
    tm = rt.shape[0]
    lane = lax.broadcasted_iota(jnp.int32, rt.shape, 1)
    lanef = lane.astype(F32)
    oh1 = jnp.where(lanef == rt[:, 0:1], 1.0, 0.0)
    oh2 = jnp.where(lanef == rt[:, 1:2], 1.0, 0.0)
    both = oh1 + oh2
    ii = lax.broadcasted_iota(jnp.int32, (tm, tm), 0)
    jj = lax.broadcasted_iota(jnp.int32, (tm, tm), 1)
    earlier = jnp.where(ii > jj, 1.0, 0.0).astype(BF16)
    prior = _dot(earlier, both.astype(BF16)) + cnt_ref[0:1, :]
    rank1 = jnp.sum(prior * oh1, axis=-1, keepdims=True)
    rank2 = jnp.sum(prior * oh2, axis=-1, keepdims=True)
    rt = jnp.where(lane == 4, rank1, rt)
    rt_ref[0] = jnp.where(lane == 5, rank2, rt)
    cnt_ref[0:1, :] = cnt_ref[0:1, :] + jnp.sum(both, axis=0, keepdims=True)


def _merge(proj, odn, ohg, xc, modsel, wdn, whg, wout, gdn, ghg, gffn, wr, br, n_ctx_tiles, tm):
    wr_hi = wr.astype(BF16)
    wr_lo = (wr - wr_hi.astype(F32)).astype(BF16)
    bsz, n, d = xc.shape
    hv = HEADS * DK
    tok = lambda w, j: pl.BlockSpec((1, tm, w), lambda b, i: (b, i, j))
    full = lambda a: pl.BlockSpec(a.shape, lambda b, i: (0,) * a.ndim)
    return pl.pallas_call(
        _merge_kernel,
        grid=(bsz, n // tm),
        in_specs=[tok(2 * d, C_GATES // (2 * d)), tok(hv, C_DNZ // hv), tok(hv, C_HGG // hv),
                  tok(hv, 0), tok(hv, 0), tok(d, 0),
                  pl.BlockSpec((1, 1, modsel.shape[-1]),
                               lambda b, i: (2 * b + (i >= n_ctx_tiles).astype(jnp.int32), 0, 0)),
                  full(wdn), full(whg), full(wout), full(gdn), full(ghg), full(gffn),
                  full(wr_hi), full(wr_lo), full(br)],
        out_specs=[tok(d, 0), tok(d, 0), tok(128, 0),
                   pl.BlockSpec((8, 128), lambda b, i: (0, 0))],
        out_shape=[jax.ShapeDtypeStruct((bsz, n, d), F32),
                   jax.ShapeDtypeStruct((bsz, n, d), F32),
                   jax.ShapeDtypeStruct((bsz, n, 128), F32),
                   jax.ShapeDtypeStruct((8, 128), F32)],
        scratch_shapes=[pltpu.VMEM((tm, d), BF16)],
        compiler_params=_cparams(("arbitrary", "arbitrary")),
        name="merge",
    )(proj, proj, proj, odn, ohg, xc, modsel, wdn, whg, wout, gdn, ghg, gffn, wr_hi, wr_lo, br)


def _dispatch_kernel(dest_ref, h_ref, init_ref, rows_ref, sem, *, tile):
    del init_ref
    base = pl.program_id(0) * tile

    def row_copy(r, d):
        return pltpu.make_async_copy(h_ref.at[pl.ds(r, 1)], rows_ref.at[pl.ds(d, 1)], sem)

    def issue(r, c):
        t = base + r
        row_copy(r, dest_ref[TOP_K * t]).start()
        row_copy(r, dest_ref[TOP_K * t + 1]).start()
        return c

    lax.fori_loop(0, tile, issue, 0, unroll=ROW_DMA_UNROLL)

    def drain(r, c):
        row_copy(0, 0).wait()
        row_copy(0, 0).wait()
        return c

    lax.fori_loop(0, tile, drain, 0, unroll=ROW_DMA_UNROLL)


def _dispatch(dest, h2, n_rows, tile):
    n_tok, d = h2.shape
    return pl.pallas_call(
        functools.partial(_dispatch_kernel, tile=tile),
        grid_spec=pltpu.PrefetchScalarGridSpec(
            num_scalar_prefetch=1,
            grid=(n_tok // tile,),
            in_specs=[pl.BlockSpec((tile, d), lambda i, dr: (i, 0)),
                      pl.BlockSpec(memory_space=pl.ANY)],
            out_specs=pl.BlockSpec(memory_space=pl.ANY),
            scratch_shapes=[pltpu.SemaphoreType.DMA(())]),
        out_shape=jax.ShapeDtypeStruct((n_rows, d), F32),
        input_output_aliases={2: 0},
        compiler_params=pltpu.CompilerParams(dimension_semantics=("arbitrary",),
                                             has_side_effects=True),
        name="moe_dispatch",
    )(dest, h2, jnp.zeros((n_rows, d), F32))


def _expert_kernel(be_ref, first_ref, nused_ref, x_ref, wg_ref, wu_ref, wd_ref, o_ref,
                   wg_s, wu_s, wd_s):
    i = pl.program_id(0)

    @pl.when(first_ref[i] == 1)
    def _():
        wg_s[...] = wg_ref[0].astype(BF16)
        wu_s[...] = wu_ref[0].astype(BF16)
        wd_s[...] = wd_ref[0].astype(BF16)

    @pl.when(i < nused_ref[0])
    def _():
        x = x_ref[...].astype(BF16)
        a = _silu(_dot(x, wg_s[...])) * _dot(x, wu_s[...])
        o_ref[...] = _dot(a.astype(BF16), wd_s[...])

    @pl.when(i >= nused_ref[0])
    def _():
        o_ref[...] = jnp.zeros(o_ref.shape, F32)


def _experts(block_expert, first, n_used, rows, wg, wu, wd):
    n_rows, d = rows.shape
    de = wg.shape[-1]
    return pl.pallas_call(
        _expert_kernel,
        grid_spec=pltpu.PrefetchScalarGridSpec(
            num_scalar_prefetch=3,
            grid=(n_rows // MOE_BLOCK,),
            in_specs=[pl.BlockSpec((MOE_BLOCK, d), lambda i, be, f, nu: (jnp.minimum(i, nu[0] - 1), 0)),
                      pl.BlockSpec((1, d, de), lambda i, be, f, nu: (be[i], 0, 0)),
                      pl.BlockSpec((1, d, de), lambda i, be, f, nu: (be[i], 0, 0)),
                      pl.BlockSpec((1, de, d), lambda i, be, f, nu: (be[i], 0, 0))],
            out_specs=pl.BlockSpec((MOE_BLOCK, d), lambda i, be, f, nu: (i, 0)),
            scratch_shapes=[pltpu.VMEM((d, de), BF16), pltpu.VMEM((d, de), BF16),
                            pltpu.VMEM((de, d), BF16)]),
        out_shape=jax.ShapeDtypeStruct((n_rows, d), F32),
        compiler_params=_cparams(("arbitrary",)),
        name="moe_experts",
    )(block_expert, first, n_used, rows, wg, wu, wd)


def _combine_kernel(dest_ref, y_ref, rt_ref, x_ref, mod_ref, gfin_ref, o_ref, buf0, buf1, sem,
                    *, tile, final):
    d = x_ref.shape[-1]
    base = (pl.program_id(0) * pl.num_programs(1) + pl.program_id(1)) * tile

    def row_copy(src, buf, r):
        return pltpu.make_async_copy(y_ref.at[pl.ds(src, 1)], buf.at[pl.ds(r, 1)], sem)

    def issue(r, c):
        t = base + r
        row_copy(dest_ref[TOP_K * t], buf0, r).start()
        row_copy(dest_ref[TOP_K * t + 1], buf1, r).start()
        return c

    lax.fori_loop(0, tile, issue, 0, unroll=ROW_DMA_UNROLL)

    def drain(r, c):
        row_copy(0, buf0, 0).wait()
        row_copy(0, buf1, 0).wait()
        return c

    lax.fori_loop(0, tile, drain, 0, unroll=ROW_DMA_UNROLL)

    rt = rt_ref[0]
    f = rt[:, 2:3] * buf0[...] + rt[:, 3:4] * buf1[...]
    x = x_ref[0] + mod_ref[0][:, 5 * d:6 * d] * f
    if final:
        x = x * lax.rsqrt(jnp.mean(x * x, axis=-1, keepdims=True) + EPS) * gfin_ref[...]
    o_ref[0] = x


def _combine(dest, y, route, xc, modsel, g_final, n_ctx_tiles, tm, final):
    bsz, n, d = xc.shape
    tok = lambda w: pl.BlockSpec((1, tm, w), lambda b, i, dr: (b, i, 0))
    return pl.pallas_call(
        functools.partial(_combine_kernel, tile=tm, final=final),
        grid_spec=pltpu.PrefetchScalarGridSpec(
            num_scalar_prefetch=1,
            grid=(bsz, n // tm),
            in_specs=[pl.BlockSpec(memory_space=pl.ANY), tok(128), tok(d),
                      pl.BlockSpec((1, 1, modsel.shape[-1]),
                                   lambda b, i, dr: (2 * b + (i >= n_ctx_tiles).astype(jnp.int32), 0, 0)),
                      pl.BlockSpec((1, d), lambda b, i, dr: (0, 0))],
            out_specs=tok(d),
            scratch_shapes=[pltpu.VMEM((tm, d), F32), pltpu.VMEM((tm, d), F32),
                            pltpu.SemaphoreType.DMA(())]),
        out_shape=jax.ShapeDtypeStruct((bsz, n, d), F32),
        compiler_params=_cparams(("arbitrary", "arbitrary")),
        name="moe_combine",
    )(dest, y, route, xc, modsel, g_final)


def _routing_tables(route, counts, n_tok):
    e_flat = route[:, :TOP_K].astype(jnp.int32).reshape(n_tok * TOP_K)
    rank = route[:, 4:4 + TOP_K].astype(jnp.int32).reshape(n_tok * TOP_K)
    counts = counts[0, :N_EXPERTS].astype(jnp.int32)
    padded = (counts + MOE_BLOCK - 1) // MOE_BLOCK * MOE_BLOCK
    pad_end = jnp.cumsum(padded)
    pad_start = pad_end - padded
    dest = pad_start[e_flat] + rank
    n_blocks = (n_tok * TOP_K + N_EXPERTS * (MOE_BLOCK - 1) + MOE_BLOCK - 1) // MOE_BLOCK
    block_row = jnp.arange(n_blocks, dtype=jnp.int32) * MOE_BLOCK
    block_expert = jnp.minimum(
        jnp.sum((pad_end[None, :] <= block_row[:, None]).astype(jnp.int32), axis=1),
        N_EXPERTS - 1).astype(jnp.int32)
    first = jnp.concatenate([jnp.ones((1,), jnp.int32),
                             (block_expert[1:] != block_expert[:-1]).astype(jnp.int32)])
    n_used = (pad_end[-1] // MOE_BLOCK).astype(jnp.int32).reshape(1)
    return dest.astype(jnp.int32), block_expert, first, n_used, n_blocks


def kernel(x, c, ctx, c_ctx, w_ada, b_ada, g_mix, g_ffn, g_final, w_in, dn_conv, dn_a_log,
           dn_dt_bias, dn_norm_g, hg_lb_logits, hg_norm_g, w_br_dn, w_br_hg, w_out,
           w_router_grp, b_router_grp, w_router_exp, b_router_exp, w_exp_gate, w_exp_up,
           w_exp_down):
    bsz, n_lat, d = x.shape
    n_ctx = ctx.shape[1]
    depth = w_ada.shape[0]
    n = n_ctx + n_lat
    n_tok = bsz * n
    tm = 256
    n_ctx_tiles = n_ctx // tm
    hk = HEADS * DK

    lb_w = jax.nn.softmax(hg_lb_logits.astype(F32), axis=0)
    lower_bounds = jnp.cumsum(lb_w, axis=0) - lb_w[0]

    cc = jnp.zeros((16, d), F32).at[:bsz].set(c).at[bsz].set(c_ctx)
    mod_all = _modulation(cc, w_ada, b_ada)

    xc = jnp.concatenate([ctx, x], axis=1)

    o_qkv, o_z, o_b, o_a = 0, 3 * hk, 4 * hk, 4 * hk + 2 * HEADS
    o_hq = o_a + 2 * HEADS
    o_hf, o_hi, o_hg, o_gt = o_hq + hk, o_hq + 3 * hk, o_hq + 4 * hk, o_hq + 5 * hk

    for layer in range(depth):
        final = layer == depth - 1
        mod = mod_all[layer]
        modsel = jnp.stack([jnp.broadcast_to(mod[bsz], (bsz, 6 * d)), mod[:bsz]],
                           axis=1).reshape(2 * bsz, 1, 6 * d)
        wl = w_in[layer]
        w_main = jnp.concatenate([wl[:, o_gt:], wl[:, o_qkv:o_b], wl[:, o_hq:o_gt]],
                                 axis=1).astype(BF16)
        wb = wl[:, o_b:o_a].reshape(d, 2, HEADS)
        wa = wl[:, o_a:o_hq].reshape(d, 2, HEADS)
        w_small_t = jnp.concatenate([wb, wa, jnp.zeros((d, 4, HEADS), F32)], axis=1)
        w_small_t = w_small_t.transpose(2, 1, 0).reshape(8 * HEADS, d)
        pad4 = jnp.zeros((HEADS, 4, 1), F32)
        zero2 = jnp.zeros((HEADS, 2, 1), F32)
        alog = jnp.concatenate([zero2, dn_a_log[layer].T[:, :, None], pad4], axis=1)
        dtb = jnp.concatenate([zero2, dn_dt_bias[layer].T[:, :, None], pad4], axis=1)

        if layer == 0:
            proj, small = _projection(xc, modsel, g_mix[layer][None, :], w_main, w_small_t,
                                      n_ctx_tiles, tm)
        else:
            xc, proj, small = _combine_projection(
                dest, y, route, x_mid, modsel_prev, modsel, g_mix[layer][None, :], w_main,
                w_small_t, n_ctx_tiles, tm)
        odn = _deltanet(proj, small, dn_conv[layer], alog, dtb, n_ctx)

        ohg = _hgrn2(proj, lower_bounds[layer][None, :], n_ctx)

        wr = jnp.zeros((d, 128), F32).at[:, :N_GROUPS].set(w_router_grp[layer])
        wr = wr.at[:, N_GROUPS:N_GROUPS + N_EXPERTS].set(w_router_exp[layer])
        br = jnp.zeros((1, 128), F32).at[0, :N_GROUPS].set(b_router_grp[layer])
        br = br.at[0, N_GROUPS:N_GROUPS + N_EXPERTS].set(b_router_exp[layer])
        x_mid, h2, route, counts = _merge(
            proj, odn, ohg, xc, modsel, w_br_dn[layer].astype(BF16), w_br_hg[layer].astype(BF16),
            w_out[layer].astype(BF16), dn_norm_g[layer][None, :], hg_norm_g[layer][None, :],
            g_ffn[layer][None, :], wr, br, n_ctx_tiles, tm)

        dest, block_expert, first, n_used, n_blocks = _routing_tables(
            route.reshape(n_tok, 128), counts, n_tok)
        rows = _dispatch(dest, h2.reshape(n_tok, d), n_blocks * MOE_BLOCK, tm)
        y = _experts(block_expert + layer * N_EXPERTS, first, n_used, rows,
                     w_exp_gate.reshape(depth * N_EXPERTS, d, -1),
                     w_exp_up.reshape(depth * N_EXPERTS, d, -1),
                     w_exp_down.reshape(depth * N_EXPERTS, -1, d))
        modsel_prev = modsel

    out = _combine(dest, y, route, x_mid, modsel, g_final[None, :], n_ctx_tiles, tm, True)
    return out[:, n_ctx:]
```

```python
import functools

import jax
import jax.numpy as jnp
from jax import lax
from jax.experimental import pallas as pl
from jax.experimental.pallas import tpu as pltpu

F32 = jnp.float32
BF16 = jnp.bfloat16
HI = lax.Precision.HIGHEST

GRID_W = 64
CHUNK = 64
EPS = 1e-6
F_TINY = 1e-30
HEADS = 4
DK = 128
N_GROUPS = 4
EXPERTS_PER_GROUP = 8
N_EXPERTS = 32
TOP_K = 2
MOE_BLOCK = 512
SUB = 16
EXP_CLAMP = 60.0
DN_INTRA_UNROLL = 3
HG_STEP_UNROLL = 4
MERGE_COLS = 256
ROW_DMA_UNROLL = 8

C_GATES = 0
C_DNQKV = 2048
C_DNZ = 3584
C_HGQ = 4096
C_HGF = 4608
C_HGI = 5632
C_HGG = 6144
N_MAIN = 6656

VMEM_LIMIT = 56 * 1024 * 1024


def _cparams(sem):
    return pltpu.CompilerParams(dimension_semantics=sem, vmem_limit_bytes=VMEM_LIMIT)


def _silu(x):
    return x * jax.nn.sigmoid(x)


def _softplus(x):
    return jnp.maximum(x, 0.0) + jnp.log1p(jnp.exp(-jnp.abs(x)))


def _dot(a, b, prec=None):
    return jnp.dot(a, b, precision=prec, preferred_element_type=F32)


def _dot_nt(a, b, prec=None):
    return lax.dot_general(a, b, (((1,), (1,)), ((), ())), precision=prec,
                           preferred_element_type=F32)


def _dot_tn(a, b, prec=None):
    return lax.dot_general(a, b, (((0,), (0,)), ((), ())), precision=prec,
                           preferred_element_type=F32)


def _bdot(a, b):
    return _dot(a.astype(BF16), b.astype(BF16))


def _bdot_nt(a, b):
    return _dot_nt(a.astype(BF16), b.astype(BF16))


def _bdot_tn(a, b):
    return _dot_tn(a.astype(BF16), b.astype(BF16))


def _mod_kernel(cc_ref, w_ref, b_ref, o_ref):
    s = _silu(cc_ref[...])
    o_ref[0] = _dot(s, w_ref[0], HI) + b_ref[0]


def _modulation(cc, w_ada, b_ada):
    depth, d, n6 = w_ada.shape
    tn = n6 // 4
    return pl.pallas_call(
        _mod_kernel,
        grid=(depth, n6 // tn),
        in_specs=[pl.BlockSpec((16, d), lambda l, j: (0, 0)),
                  pl.BlockSpec((1, d, tn), lambda l, j: (l, 0, j)),
                  pl.BlockSpec((1, 1, tn), lambda l, j: (l, 0, j))],
        out_specs=pl.BlockSpec((1, 16, tn), lambda l, j: (l, 0, j)),
        out_shape=jax.ShapeDtypeStruct((depth, 16, n6), F32),
        compiler_params=_cparams(("parallel", "parallel")),
        name="modulation",
    )(cc, w_ada, b_ada.reshape(depth, 1, n6))


def _proj_kernel(x_ref, mod_ref, g_ref, w_ref, ws_ref, p_ref, s_ref):
    d = x_ref.shape[-1]
    x = x_ref[0]
    m = mod_ref[0]
    h = x * lax.rsqrt(jnp.mean(x * x, axis=-1, keepdims=True) + EPS) * g_ref[...]
    h = h * (1.0 + m[:, d:2 * d]) + m[:, :d]
    p_ref[0] = _dot(h.astype(BF16), w_ref[...])
    s_ref[0] = _dot_nt(ws_ref[...], h, HI)


def _combine_proj_kernel(dest_ref, y_ref, rt_ref, x_ref, modp_ref, mod_ref, g_ref, w_ref, ws_ref,
                         xo_ref, p_ref, s_ref, bufs, sems, *, tile):
    d = x_ref.shape[-1]
    steps_per_batch = pl.num_programs(1)
    step = pl.program_id(0) * steps_per_batch + pl.program_id(1)
    last = pl.num_programs(0) * steps_per_batch - 1
    slot = step % 2

    def row_copy(src, sl, k, r):
        return pltpu.make_async_copy(y_ref.at[pl.ds(src, 1)], bufs.at[sl, k, pl.ds(r, 1)],
                                     sems.at[sl])

    def issue_row(base, sl, r):
        t = base + r
        row_copy(dest_ref[TOP_K * t], sl, 0, r).start()
        row_copy(dest_ref[TOP_K * t + 1], sl, 1, r).start()

    def drain(sl):
        def wait_row(r, c):
            row_copy(0, sl, 0, 0).wait()
            row_copy(0, sl, 1, 0).wait()
            return c

        lax.fori_loop(0, tile, wait_row, 0, unroll=ROW_DMA_UNROLL)

    @pl.when(step == 0)
    def _():
        def first(r, c):
            issue_row(0, 0, r)
            return c

        lax.fori_loop(0, tile, first, 0, unroll=ROW_DMA_UNROLL)

    drain(slot)
    rt = rt_ref[0]
    f = rt[:, 2:3] * bufs[slot, 0] + rt[:, 3:4] * bufs[slot, 1]
    x = x_ref[0] + modp_ref[0][:, 5 * d:6 * d] * f
    xo_ref[0] = x

    next_base = jnp.minimum(step + 1, last) * tile
    for r in range(tile):
        issue_row(next_base, 1 - slot, r)

    m = mod_ref[0]
    h = x * lax.rsqrt(jnp.mean(x * x, axis=-1, keepdims=True) + EPS) * g_ref[...]
    h = h * (1.0 + m[:, d:2 * d]) + m[:, :d]
    p_ref[0] = _dot(h.astype(BF16), w_ref[...])
    s_ref[0] = _dot_nt(ws_ref[...], h, HI)

    @pl.when(step == last)
    def _():
        drain(1 - slot)


def _combine_projection(dest, y, route, x_mid, modsel_prev, modsel, g, w_main, w_small_t,
                        n_ctx_tiles, tm):
    bsz, n, d = x_mid.shape
    n_main = w_main.shape[1]
    tok = lambda w: pl.BlockSpec((1, tm, w), lambda b, i, dr: (b, i, 0))
    mod_spec = pl.BlockSpec(
        (1, 1, modsel.shape[-1]),
        lambda b, i, dr: (2 * b + (i >= n_ctx_tiles).astype(jnp.int32), 0, 0))
    return pl.pallas_call(
        functools.partial(_combine_proj_kernel, tile=tm),
        grid_spec=pltpu.PrefetchScalarGridSpec(
            num_scalar_prefetch=1,
            grid=(bsz, n // tm),
            in_specs=[pl.BlockSpec(memory_space=pl.ANY), tok(128), tok(d), mod_spec, mod_spec,
                      pl.BlockSpec((1, d), lambda b, i, dr: (0, 0)),
                      pl.BlockSpec((d, n_main), lambda b, i, dr: (0, 0),
                                   pipeline_mode=pl.Buffered(1)),
                      pl.BlockSpec((32, d), lambda b, i, dr: (0, 0))],
            out_specs=[tok(d), tok(n_main),
                       pl.BlockSpec((1, 32, tm), lambda b, i, dr: (b, 0, i))],
            scratch_shapes=[pltpu.VMEM((2, TOP_K, tm, d), F32), pltpu.SemaphoreType.DMA((2,))]),
        out_shape=[jax.ShapeDtypeStruct((bsz, n, d), F32),
                   jax.ShapeDtypeStruct((bsz, n, n_main), F32),
                   jax.ShapeDtypeStruct((bsz, 32, n), F32)],
        compiler_params=_cparams(("arbitrary", "arbitrary")),
        name="combine_projection",
    )(dest, y, route, x_mid, modsel_prev, modsel, g, w_main, w_small_t)


def _projection(xc, modsel, g, w_main, w_small_t, n_ctx_tiles, tm):
    bsz, n, d = xc.shape
    n_main = w_main.shape[1]
    return pl.pallas_call(
        _proj_kernel,
        grid=(bsz, n // tm),
        in_specs=[pl.BlockSpec((1, tm, d), lambda b, i: (b, i, 0)),
                  pl.BlockSpec((1, 1, modsel.shape[-1]),
                               lambda b, i: (2 * b + (i >= n_ctx_tiles).astype(jnp.int32), 0, 0)),
                  pl.BlockSpec((1, d), lambda b, i: (0, 0)),
                  pl.BlockSpec((d, n_main), lambda b, i: (0, 0), pipeline_mode=pl.Buffered(1)),
                  pl.BlockSpec((32, d), lambda b, i: (0, 0))],
        out_specs=[pl.BlockSpec((1, tm, n_main), lambda b, i: (b, i, 0)),
                   pl.BlockSpec((1, 32, tm), lambda b, i: (b, 0, i))],
        out_shape=[jax.ShapeDtypeStruct((bsz, n, n_main), F32),
                   jax.ShapeDtypeStruct((bsz, 32, n), F32)],
        compiler_params=_cparams(("parallel", "parallel")),
        name="projection",
    )(xc, modsel, g, w_main, w_small_t)


def _tri_masks(fwd):
    ii = lax.broadcasted_iota(jnp.int32, (CHUNK, CHUNK), 0)
    jj = lax.broadcasted_iota(jnp.int32, (CHUNK, CHUNK), 1)
    if fwd:
        return ii >= jj, ii > jj
    return ii <= jj, ii < jj


def _unit_tri_solve(a, rhs, mm):
    ii = lax.broadcasted_iota(jnp.int32, a[0].shape, 0)
    jj = lax.broadcasted_iota(jnp.int32, a[0].shape, 1)
    same = (ii // SUB) == (jj // SUB)
    eye = jnp.where(ii == jj, 1.0, 0.0)
    each = lambda f, *xs: [f(*t) for t in zip(*xs)]
    axpy = lambda x, y: each(lambda xi, yi: xi + mm(xi, yi), x, y)
    dm = each(lambda t: jnp.where(same, t, 0.0), a)
    lm = each(lambda t: jnp.where(same, 0.0, t), a)
    d2 = each(mm, dm, dm)
    d4 = each(mm, d2, d2)
    d8 = each(mm, d4, d4)
    p = axpy(each(lambda t: eye - t, dm), d2)
    p = axpy(p, d4)
    td = axpy(p, d8)
    m = each(mm, td, lm)
    m2 = each(mm, m, m)
    t1 = each(mm, td, rhs)
    t2 = each(lambda x, y: x + mm(y, x), t1, m2)
    return each(lambda x, y: x - mm(y, x), t2, m)


def _chunk_schedule(s, n_ctx_chunks, n_chunks):
    cf = s
    cb = jnp.where(s < n_ctx_chunks, n_ctx_chunks - 1 - s, n_chunks + n_ctx_chunks - 1 - s)
    return pl.multiple_of(cf * CHUNK, CHUNK), pl.multiple_of(cb * CHUNK, CHUNK)


def _seq_cumsum(x, fwd):
    n = x.shape[0]
    pos = lax.broadcasted_iota(jnp.int32, x.shape, 0) % CHUNK
    s = 1
    while s < CHUNK:
        if fwd:
            x = x + jnp.where(pos >= s, pltpu.roll(x, s, 0), 0.0)
        else:
            x = x + jnp.where(pos < CHUNK - s, pltpu.roll(x, n - s, 0), 0.0)
        s *= 2
    return x


def _dn_intra(q, k, v, beta, gc, fwds):
    m = q[0].shape[0]
    ii = lax.broadcasted_iota(jnp.int32, (m, m), 0)
    jj = lax.broadcasted_iota(jnp.int32, (m, m), 1)
    blk = ii // CHUNK
    same = blk == (jj // CHUNK)
    is_fwd = functools.reduce(lambda x, y: x | y,
                              [blk == g for g, f in enumerate(fwds) if f], blk < 0)
    incl = same & ((is_fwd & (ii >= jj)) | (~is_fwd & (ii <= jj)))
    strict = incl & (ii != jj)
    wide = lambda t: jnp.concatenate([t] * (m // DK), axis=1)
    tall = lambda t: jnp.concatenate([t] * (m // DK), axis=0)
    each = lambda f, *xs: [f(*t) for t in zip(*xs)]
    kk = each(_bdot_nt, k, k)
    qk = each(_bdot_nt, q, k)
    dec = each(lambda g: jnp.where(incl, jnp.exp(jnp.where(incl, wide(g) - tall(g.T), 0.0)), 0.0), gc)
    a = each(lambda x, b, d: jnp.where(strict, x * wide(b) * d, 0.0), kk, beta, dec)
    eg = each(jnp.exp, gc)
    rhs = each(lambda vi, ki, b, e: jnp.concatenate([vi * b, ki * (b * e)], axis=1), v, k, beta, eg)
    sol = _unit_tri_solve(a, rhs, _bdot)

    def to_end(g):
        rows = [g[i * CHUNK + (CHUNK - 1 if f else 0)][None, :] for i, f in enumerate(fwds)]
        return jnp.concatenate([jnp.broadcast_to(r, (CHUNK, DK)) for r in rows], axis=0) - g

    n_stack = len(fwds)
    ri = lax.broadcasted_iota(jnp.int32, (n_stack * DK, m), 0) // DK
    ci = lax.broadcasted_iota(jnp.int32, (n_stack * DK, m), 1) // CHUNK
    kdt_bd = each(lambda ki, g: jnp.where(
        ri == ci, jnp.concatenate([(ki * jnp.exp(to_end(g))).T] * n_stack, axis=0), 0.0), k, gc)
    att = each(lambda x, d: jnp.where(incl, x * d, 0.0), qk, dec)
    att_uw = each(_bdot, att, sol)
    kd_uw = each(_bdot, kdt_bd, sol)
    q_eff = each(lambda x, e, aw: x * e - aw[:, DK:], q, eg, att_uw)
    return [(aw[:, :DK], qe, kw[:, :DK], kw[:, DK:]) for aw, qe, kw in zip(att_uw, q_eff, kd_uw)]


def _dn_kernel(q_ref, k_ref, v_ref, cq_ref, ck_ref, cv_ref, sm_ref, alog_ref, dtb_ref,
               o_ref, qs, ks, vs, bfs, bbs, gfs, gbs, qes, cs, ns, *, n_ctx):
    n = q_ref.shape[1]
    row = lax.broadcasted_iota(jnp.int32, (n, DK), 0)
    seg_first = (row == 0) | (row == n_ctx)
    seg_last = (row == n_ctx - 1) | (row == n - 1)

    def conv_act(x_ref, c_ref):
        x = x_ref[0]
        w = c_ref[...]
        xm = jnp.where(seg_first, 0.0, pltpu.roll(x, 1, 0))
        xp = jnp.where(seg_last, 0.0, pltpu.roll(x, n - 1, 0))
        return _silu(xm * w[0:1] + x * w[1:2] + xp * w[2:3])

    def l2n(y):
        return y * lax.rsqrt(jnp.sum(y * y, axis=-1, keepdims=True) + EPS)

    qs[...] = l2n(conv_act(q_ref, cq_ref)) * (DK ** -0.5)
    ks[...] = l2n(conv_act(k_ref, ck_ref))
    vs[...] = conv_act(v_ref, cv_ref)

    sm = sm_ref[0]
    beta = jax.nn.sigmoid(sm)
    g = -jnp.exp(alog_ref[0]) * _softplus(sm + dtb_ref[0])
    pos = lax.broadcasted_iota(jnp.int32, sm.shape, 1) % CHUNK
    pre, suf = g, g
    s = 1
    while s < CHUNK:
        pre = pre + jnp.where(pos >= s, pltpu.roll(pre, s, 1), 0.0)
        suf = suf + jnp.where(pos < CHUNK - s, pltpu.roll(suf, n - s, 1), 0.0)
        s *= 2

    def col(r):
        return jnp.broadcast_to(r, (DK, n)).T

    bfs[...] = col(beta[0:1])
    bbs[...] = col(beta[1:2])
    gfs[...] = col(pre[2:3])
    gbs[...] = col(suf[3:4])

    n_chunks = n // CHUNK
    n_ctx_chunks = n_ctx // CHUNK

    dirs = ((True, bfs, gfs), (False, bbs, gbs))

    def intra(c, carry):
        starts = [pl.multiple_of((c * DN_INTRA_UNROLL + j) * 2 * CHUNK, 2 * CHUNK)
                  for j in range(DN_INTRA_UNROLL)]
        pairs = [pl.ds(r0, 2 * CHUNK) for r0 in starts]
        stack = lambda t: jnp.concatenate([t[:CHUNK], t[:CHUNK], t[CHUNK:], t[CHUNK:]], axis=0)
        both = lambda tf, tb: jnp.concatenate(
            [tf[:CHUNK], tb[:CHUNK], tf[CHUNK:], tb[CHUNK:]], axis=0)
        results = _dn_intra(
            [stack(qs[p, :]) for p in pairs], [stack(ks[p, :]) for p in pairs],
            [stack(vs[p, :]) for p in pairs],
            [both(bfs[p, :], bbs[p, :]) for p in pairs],
            [both(gfs[p, :], gbs[p, :]) for p in pairs], (True, False, True, False))
        for r0, (o0, q_eff, c_all, n_all) in zip(starts, results):
            for g in range(4):
                di = g % 2
                rg = r0 + (g // 2) * CHUNK
                sl = pl.ds(rg, CHUNK)
                sl2 = pl.ds(pl.multiple_of(2 * rg, 2 * CHUNK), DK)
                if di == 0:
                    o_ref[0, sl, :] = o0[g * CHUNK:(g + 1) * CHUNK] + o0[(g + 1) * CHUNK:(g + 2) * CHUNK]
                qes[di, sl, :] = q_eff[g * CHUNK:(g + 1) * CHUNK].astype(BF16)
                cs[di, sl2, :] = c_all[g * DK:(g + 1) * DK]
                ns[di, sl2, :] = n_all[g * DK:(g + 1) * DK].astype(BF16)
        return carry

    lax.fori_loop(0, n_chunks // (2 * DN_INTRA_UNROLL), intra, 0)

    def step(s, states):
        rows = _chunk_schedule(s, n_ctx_chunks, n_chunks)
        sls = [pl.ds(r0, CHUNK) for r0 in rows]
        sl2s = [pl.ds(pl.multiple_of(2 * r0, 2 * CHUNK), DK) for r0 in rows]
        stb = [st.astype(BF16) for st in states]
        corr = [_dot(ns[di, sl2s[di], :], stb[di]) for di in range(2)]
        outs = [_dot(qes[di, sls[di], :], stb[di]) for di in range(2)]
        new_states = []
        for di, (fwd, _, g_scr) in enumerate(dirs):
            g_tot = g_scr[pl.ds(rows[di] + (CHUNK - 1 if fwd else 0), 1), :]
            new_states.append(states[di] * jnp.exp(g_tot) + cs[di, sl2s[di], :] - corr[di])
            o_ref[0, sls[di], :] = o_ref[0, sls[di], :] + outs[di]
        return tuple(new_states)

    zero = jnp.zeros((DK, DK), F32)
    lax.fori_loop(0, n_chunks, step, (zero, zero))


def _deltanet(proj, small, conv_w, alog, dtb, n_ctx):
    bsz, n, _ = proj.shape
    qb = C_DNQKV // DK
    seq = lambda off: pl.BlockSpec((1, n, DK), lambda b, h: (b, 0, off + h))
    cw = lambda off: pl.BlockSpec((3, DK), lambda b, h: (0, off + h))
    return pl.pallas_call(
        functools.partial(_dn_kernel, n_ctx=n_ctx),
        grid=(bsz, HEADS),
        in_specs=[seq(qb), seq(qb + HEADS), seq(qb + 2 * HEADS),
                  cw(0), cw(HEADS), cw(2 * HEADS),
                  pl.BlockSpec((1, 8, n), lambda b, h: (b, h, 0)),
                  pl.BlockSpec((1, 8, 1), lambda b, h: (h, 0, 0)),
                  pl.BlockSpec((1, 8, 1), lambda b, h: (h, 0, 0))],
        out_specs=pl.BlockSpec((1, n, DK), lambda b, h: (b, 0, h)),
        out_shape=jax.ShapeDtypeStruct((bsz, n, HEADS * DK), F32),
        scratch_shapes=[pltpu.VMEM((n, DK), F32) for _ in range(7)] + [
            pltpu.VMEM((2, n, DK), BF16),
            pltpu.VMEM((2, 2 * n, DK), F32),
            pltpu.VMEM((2, 2 * n, DK), BF16)],
        compiler_params=_cparams(("parallel", "parallel")),
        name="deltanet",
    )(proj, proj, proj, conv_w, conv_w, conv_w, small, alog, dtb)


def _hg_intra(q, zf, zb, v, lb):
    fwds = (True, False, True, False)
    n_stack = len(fwds)
    nsub = CHUNK // SUB

    def gate(z):
        f = lb + (1.0 - lb) * jax.nn.sigmoid(z)
        return (1.0 - lb) * jax.nn.sigmoid(-z), jnp.log(jnp.maximum(f, F_TINY))

    kf_f, lf_f = gate(zf)
    kf_b, lf_b = gate(zb)
    both = lambda tf, tb: jnp.concatenate([tf[:CHUNK], tb[:CHUNK], tf[CHUNK:], tb[CHUNK:]], axis=0)
    stack = lambda t: both(t, t)
    q_st = stack(_silu(q))
    v_st = stack(v)
    kf = both(kf_f, kf_b)
    gc = both(_seq_cumsum(lf_f, True), _seq_cumsum(lf_b, False))

    m = n_stack * CHUNK
    ii = lax.broadcasted_iota(jnp.int32, (m, m), 0)
    jj = lax.broadcasted_iota(jnp.int32, (m, m), 1)
    blk = ii // CHUNK
    is_fwd = functools.reduce(lambda x, y: x | y,
                              [blk == g for g, f in enumerate(fwds) if f], blk < 0)
    incl = (blk == (jj // CHUNK)) & ((is_fwd & (ii >= jj)) | (~is_fwd & (ii <= jj)))

    def ref_row(g, i):
        r = g * CHUNK + i * SUB + (0 if fwds[g] else SUB - 1)
        return gc[r:r + 1, :]

    rep = lambda r, k: jnp.broadcast_to(r, (k, DK))
    qe = q_st * jnp.exp(gc - jnp.concatenate(
        [rep(ref_row(g, i), SUB) for g in range(n_stack) for i in range(nsub)], axis=0))
    scores = []
    for i in range(nsub):
        ref = jnp.concatenate([rep(ref_row(g, i), CHUNK) for g in range(n_stack)], axis=0)
        ke = kf * jnp.exp(jnp.minimum(ref - gc, EXP_CLAMP))
        lhs = jnp.concatenate([qe[g * CHUNK + i * SUB:g * CHUNK + (i + 1) * SUB]
                               for g in range(n_stack)], axis=0)
        scores.append(_bdot_nt(lhs, ke))
    att = jnp.concatenate([scores[i][g * SUB:(g + 1) * SUB]
                           for g in range(n_stack) for i in range(nsub)], axis=0)
    o0 = _bdot(jnp.where(incl, att, 0.0), v_st)

    end_rows = [gc[g * CHUNK + (CHUNK - 1 if f else 0)][None, :] for g, f in enumerate(fwds)]
    kd = kf * jnp.exp(jnp.concatenate([rep(r, CHUNK) for r in end_rows], axis=0) - gc)
    ri = lax.broadcasted_iota(jnp.int32, (n_stack * DK, m), 0) // DK
    ci = lax.broadcasted_iota(jnp.int32, (n_stack * DK, m), 1) // CHUNK
    vt_bd = jnp.where(ri == ci, jnp.concatenate([v_st.T] * n_stack, axis=0), 0.0)
    ct = _bdot(vt_bd, kd)
    return o0, q_st * jnp.exp(gc), ct, [jnp.exp(r) for r in end_rows]


def _hg_kernel(q_ref, zf_ref, zb_ref, v_ref, lb_ref, o_ref, oscan, qgs, cts, es, *, n_ctx):
    n = q_ref.shape[1]
    n_lat = n - n_ctx
    col_len = n_lat // GRID_W
    cols_per_pair = 2 * CHUNK // col_len
    lb = lb_ref[...]

    def store(r0, res):
        o0, qg, ct, decay = res
        for g in range(4):
            di = g % 2
            rg = r0 + (g // 2) * CHUNK
            sl = pl.ds(rg, CHUNK)
            if di == 0:
                oscan[sl, :] = o0[g * CHUNK:(g + 1) * CHUNK] + o0[(g + 1) * CHUNK:(g + 2) * CHUNK]
            qgs[di, sl, :] = qg[g * CHUNK:(g + 1) * CHUNK].astype(BF16)
            cts[di, pl.ds(pl.multiple_of(2 * rg, 2 * CHUNK), DK), :] = ct[g * DK:(g + 1) * DK]
            es[di, pl.ds(rg // CHUNK, 1), :] = decay[g]

    def ctx_pair(i, c):
        r0 = pl.multiple_of(i * 2 * CHUNK, 2 * CHUNK)
        sl = pl.ds(r0, 2 * CHUNK)
        store(r0, _hg_intra(q_ref[0, sl, :], zf_ref[0, sl, :], zb_ref[0, sl, :], v_ref[0, sl, :], lb))
        return c

    lax.fori_loop(0, n_ctx // (2 * CHUNK), ctx_pair, 0)

    def lat_pair(i, c):
        def load(ref):
            return jnp.concatenate(
                [ref[0, pl.ds(n_ctx + cols_per_pair * i + j, col_len, stride=GRID_W), :]
                 for j in range(cols_per_pair)], axis=0)

        r0 = pl.multiple_of(n_ctx + i * 2 * CHUNK, 2 * CHUNK)
        store(r0, _hg_intra(load(q_ref), load(zf_ref), load(zb_ref), load(v_ref), lb))
        return c

    lax.fori_loop(0, n_lat // (2 * CHUNK), lat_pair, 0)

    n_chunks = n // CHUNK
    n_ctx_chunks = n_ctx // CHUNK

    def steps(t, states):
        for j in range(HG_STEP_UNROLL):
            rows = _chunk_schedule(t * HG_STEP_UNROLL + j, n_ctx_chunks, n_chunks)
            sls = [pl.ds(r0, CHUNK) for r0 in rows]
            outs = [_dot_nt(qgs[di, sls[di], :], states[di].astype(BF16)) for di in range(2)]
            new_states = []
            for di in range(2):
                decay = es[di, pl.ds(rows[di] // CHUNK, 1), :]
                c = cts[di, pl.ds(pl.multiple_of(2 * rows[di], 2 * CHUNK), DK), :]
                new_states.append(states[di] * decay + c)
                oscan[sls[di], :] = oscan[sls[di], :] + outs[di]
            states = tuple(new_states)
        return states

    zero = jnp.zeros((DK, DK), F32)
    lax.fori_loop(0, n_chunks // HG_STEP_UNROLL, steps, (zero, zero))

    o_ref[0, pl.ds(0, n_ctx), :] = oscan[pl.ds(0, n_ctx), :]

    def grid_row(r, c):
        dst = pl.ds(pl.multiple_of(n_ctx + r * GRID_W, GRID_W), GRID_W)
        o_ref[0, dst, :] = oscan[pl.ds(n_ctx + r, GRID_W, stride=col_len), :]
        return c

    lax.fori_loop(0, col_len, grid_row, 0)


def _hgrn2(proj, lb, n_ctx):
    bsz, n, _ = proj.shape
    seq = lambda off: pl.BlockSpec((1, n, DK), lambda b, h: (b, 0, off // DK + h))
    n_chunks = n // CHUNK
    return pl.pallas_call(
        functools.partial(_hg_kernel, n_ctx=n_ctx),
        grid=(bsz, HEADS),
        in_specs=[seq(C_HGQ), seq(C_HGF), seq(C_HGF + HEADS * DK), seq(C_HGI),
                  pl.BlockSpec((1, DK), lambda b, h: (0, h))],
        out_specs=pl.BlockSpec((1, n, DK), lambda b, h: (b, 0, h)),
        out_shape=jax.ShapeDtypeStruct((bsz, n, HEADS * DK), F32),
        scratch_shapes=[pltpu.VMEM((n, DK), F32),
                        pltpu.VMEM((2, n, DK), BF16),
                        pltpu.VMEM((2, 2 * n, DK), F32),
                        pltpu.VMEM((2, -(-n_chunks // 8) * 8, DK), F32)],
        compiler_params=_cparams(("parallel", "parallel")),
        name="hgrn2",
    )(proj, proj, proj, proj, lb)


def _gated_rmsnorm(o, z, g):
    parts = []
    for h in range(HEADS):
        oh = o[:, h * DK:(h + 1) * DK]
        zh = z[:, h * DK:(h + 1) * DK]
        y = oh * lax.rsqrt(jnp.mean(oh * oh, axis=-1, keepdims=True) + EPS) * g
        parts.append(y * _silu(zh))
    return jnp.concatenate(parts, axis=1)


def _route(lg):
    lane = lax.broadcasted_iota(jnp.int32, lg.shape, 1).astype(F32)
    neg = -1e30
    big = 1e9
    is_grp = lane < N_GROUPS
    gl = jnp.where(is_grp, lg, neg)
    gmax = jnp.max(gl, axis=-1, keepdims=True)
    gsel = jnp.min(jnp.where(gl == gmax, lane, big), axis=-1, keepdims=True)
    gp = 1.0 / jnp.sum(jnp.where(is_grp, jnp.exp(gl - gmax), 0.0), axis=-1, keepdims=True)
    lo = N_GROUPS + EXPERTS_PER_GROUP * gsel
    el = jnp.where((lane >= lo) & (lane < lo + EXPERTS_PER_GROUP), lg, neg)
    m1 = jnp.max(el, axis=-1, keepdims=True)
    i1 = jnp.min(jnp.where(el == m1, lane, big), axis=-1, keepdims=True)
    el2 = jnp.where(lane == i1, neg, el)
    m2 = jnp.max(el2, axis=-1, keepdims=True)
    i2 = jnp.min(jnp.where(el2 == m2, lane, big), axis=-1, keepdims=True)
    r = jnp.exp(m2 - m1)
    g1 = gp / (1.0 + r)
    g2 = g1 * r
    out = jnp.where(lane == 0, i1 - N_GROUPS, 0.0)
    out = jnp.where(lane == 1, i2 - N_GROUPS, out)
    out = jnp.where(lane == 2, g1, out)
    return jnp.where(lane == 3, g2, out)


def _merge_kernel(gate_ref, z_ref, hg_ref, odn_ref, ohg_ref, x_ref, mod_ref, wdn_ref, whg_ref,
                  wout_ref, gdn_ref, ghg_ref, gffn_ref, wrh_ref, wrl_ref, br_ref,
                  xo_ref, h2_ref, rt_ref, cnt_ref, mix_s):
    d = x_ref.shape[-1]
    cols = [slice(j * MERGE_COLS, (j + 1) * MERGE_COLS) for j in range(d // MERGE_COLS)]
    mod = lambda k, cs: mod_ref[0, :, k * d + cs.start:k * d + cs.stop]

    @pl.when((pl.program_id(0) == 0) & (pl.program_id(1) == 0))
    def _():
        cnt_ref[...] = jnp.zeros(cnt_ref.shape, F32)

    a_dn = _gated_rmsnorm(odn_ref[0], z_ref[0], gdn_ref[...]).astype(BF16)
    a_hg = _gated_rmsnorm(ohg_ref[0], hg_ref[0], ghg_ref[...]).astype(BF16)
    for cs in cols:
        br_dn = _dot(a_dn, wdn_ref[:, cs])
        br_hg = _dot(a_hg, whg_ref[:, cs])
        g_dn = jax.nn.sigmoid(gate_ref[0, :, cs])
        g_hg = jax.nn.sigmoid(gate_ref[0, :, d + cs.start:d + cs.stop])
        mix_s[:, cs] = (g_dn * br_dn + g_hg * br_hg).astype(BF16)
    mix = mix_s[...]
    ssq = jnp.zeros((x_ref.shape[1], 1), F32)
    for cs in cols:
        x = x_ref[0, :, cs] + mod(2, cs) * _dot(mix, wout_ref[:, cs])
        xo_ref[0, :, cs] = x
        ssq = ssq + jnp.sum(x * x, axis=-1, keepdims=True)
    scale = lax.rsqrt(ssq * (1.0 / d) + EPS)
    lg = jnp.zeros(rt_ref.shape[1:], F32) + br_ref[...]
    for cs in cols:
        h2 = xo_ref[0, :, cs] * scale * gffn_ref[:, cs] * (1.0 + mod(4, cs)) + mod(3, cs)
        h2_ref[0, :, cs] = h2
        h_hi = h2.astype(BF16)
        h_lo = (h2 - h_hi.astype(F32)).astype(BF16)
        lg = lg + (_dot(h_hi, wrh_ref[cs, :]) + (_dot(h_hi, wrl_ref[cs, :]) + _dot(h_lo, wrh_ref[cs, :])))
    rt = _route(lg)
    tm = rt.shape[0]
    lane = lax.broadcasted_iota(jnp.int32, rt.shape, 1)
    lanef = lane.astype(F32)
    oh1 = jnp.where(lanef == rt[:, 0:1], 1.0, 0.0)
    oh2 = jnp.where(lanef == rt[:, 1:2], 1.0, 0.0)
    both = oh1 + oh2
    ii = lax.broadcasted_iota(jnp.int32, (tm, tm), 0)
    jj = lax.broadcasted_iota(jnp.int32, (tm, tm), 1)
    earlier = jnp.where(ii > jj, 1.0, 0.0).astype(BF16)
    prior = _dot(earlier, both.astype(BF16)) + cnt_ref[0:1, :]
    rank1 = jnp.sum(prior * oh1, axis=-1, keepdims=True)
    rank2 = jnp.sum(prior * oh2, axis=-1, keepdims=True)
    rt = jnp.where(lane == 4, rank1, rt)
    rt_ref[0] = jnp.where(lane == 5, rank2, rt)
    cnt_ref[0:1, :] = cnt_ref[0:1, :] + jnp.sum(both, axis=0, keepdims=True)


def _merge(proj, odn, ohg, xc, modsel, wdn, whg, wout, gdn, ghg, gffn, wr, br, n_ctx_tiles, tm):
    wr_hi = wr.astype(BF16)
    wr_lo = (wr - wr_hi.astype(F32)).astype(BF16)
    bsz, n, d = xc.shape
    hv = HEADS * DK
    tok = lambda w, j: pl.BlockSpec((1, tm, w), lambda b, i: (b, i, j))
    full = lambda a: pl.BlockSpec(a.shape, lambda b, i: (0,) * a.ndim)
    return pl.pallas_call(
        _merge_kernel,
        grid=(bsz, n // tm),
        in_specs=[tok(2 * d, C_GATES // (2 * d)), tok(hv, C_DNZ // hv), tok(hv, C_HGG // hv),
                  tok(hv, 0), tok(hv, 0), tok(d, 0),
                  pl.BlockSpec((1, 1, modsel.shape[-1]),
                               lambda b, i: (2 * b + (i >= n_ctx_tiles).astype(jnp.int32), 0, 0)),
                  full(wdn), full(whg), full(wout), full(gdn), full(ghg), full(gffn),
                  full(wr_hi), full(wr_lo), full(br)],
        out_specs=[tok(d, 0), tok(d, 0), tok(128, 0),
                   pl.BlockSpec((8, 128), lambda b, i: (0, 0))],
        out_shape=[jax.ShapeDtypeStruct((bsz, n, d), F32),
                   jax.ShapeDtypeStruct((bsz, n, d), F32),
                   jax.ShapeDtypeStruct((bsz, n, 128), F32),
                   jax.ShapeDtypeStruct((8, 128), F32)],
        scratch_shapes=[pltpu.VMEM((tm, d), BF16)],
        compiler_params=_cparams(("arbitrary", "arbitrary")),
        name="merge",
    )(proj, proj, proj, odn, ohg, xc, modsel, wdn, whg, wout, gdn, ghg, gffn, wr_hi, wr_lo, br)


def _expert_kernel(be_ref, first_ref, nused_ref, src_ref, h_ref, wg_ref, wu_ref, wd_ref, o_ref,
                   wg_s, wu_s, wd_s, xbuf, xs, sems):
    i = pl.program_id(0)
    n_used = nused_ref[0]
    slot = i % 2

    def row_copy(tok, sl, r):
        return pltpu.make_async_copy(h_ref.at[pl.ds(tok, 1)], xbuf.at[sl, pl.ds(r, 1)], sems.at[sl])

    def drain(sl):
        def wait_row(r, c):
            row_copy(0, sl, 0).wait()
            return c

        lax.fori_loop(0, MOE_BLOCK, wait_row, 0, unroll=ROW_DMA_UNROLL)

    @pl.when(i == 0)
    def _():
        def first_rows(r, c):
            row_copy(src_ref[r], 0, r).start()
            return c

        lax.fori_loop(0, MOE_BLOCK, first_rows, 0, unroll=ROW_DMA_UNROLL)

    @pl.when(first_ref[i] == 1)
    def _():
        wg_s[...] = wg_ref[0].astype(BF16)
        wu_s[...] = wu_ref[0].astype(BF16)
        wd_s[...] = wd_ref[0].astype(BF16)

    @pl.when(i < n_used)
    def _():
        drain(slot)
        xs[...] = xbuf[slot].astype(BF16)
        next_base = jnp.minimum(i + 1, n_used - 1) * MOE_BLOCK
        for r in range(MOE_BLOCK):
            row_copy(src_ref[next_base + r], 1 - slot, r).start()
        x = xs[...]
        a = _silu(_dot(x, wg_s[...])) * _dot(x, wu_s[...])
        o_ref[...] = _dot(a.astype(BF16), wd_s[...])

    @pl.when(i == n_used - 1)
    def _():
        drain(1 - slot)

    @pl.when(i >= n_used)
    def _():
        o_ref[...] = jnp.zeros(o_ref.shape, F32)


def _experts(block_expert, first, n_used, src, h2, n_rows, wg, wu, wd):
    d = h2.shape[-1]
    de = wg.shape[-1]
    wspec = lambda shape: pl.BlockSpec(shape, lambda i, be, f, nu, sr: (be[i], 0, 0))
    return pl.pallas_call(
        _expert_kernel,
        grid_spec=pltpu.PrefetchScalarGridSpec(
            num_scalar_prefetch=4,
            grid=(n_rows // MOE_BLOCK,),
            in_specs=[pl.BlockSpec(memory_space=pl.ANY),
                      wspec((1, d, de)), wspec((1, d, de)), wspec((1, de, d))],
            out_specs=pl.BlockSpec((MOE_BLOCK, d), lambda i, be, f, nu, sr: (i, 0)),
            scratch_shapes=[pltpu.VMEM((d, de), BF16), pltpu.VMEM((d, de), BF16),
                            pltpu.VMEM((de, d), BF16),
                            pltpu.VMEM((2, MOE_BLOCK, d), F32), pltpu.VMEM((MOE_BLOCK, d), BF16),
                            pltpu.SemaphoreType.DMA((2,))]),
        out_shape=jax.ShapeDtypeStruct((n_rows, d), F32),
        compiler_params=_cparams(("arbitrary",)),
        name="moe_experts",
    )(block_expert, first, n_used, src, h2, wg, wu, wd)


def _combine_kernel(dest_ref, y_ref, rt_ref, x_ref, mod_ref, gfin_ref, o_ref, buf0, buf1, sem,
                    *, tile, final):
    d = x_ref.shape[-1]
    base = (pl.program_id(0) * pl.num_programs(1) + pl.program_id(1)) * tile

    def row_copy(src, buf, r):
        return pltpu.make_async_copy(y_ref.at[pl.ds(src, 1)], buf.at[pl.ds(r, 1)], sem)

    def issue(r, c):
        t = base + r
        row_copy(dest_ref[TOP_K * t], buf0, r).start()
        row_copy(dest_ref[TOP_K * t + 1], buf1, r).start()
        return c

    lax.fori_loop(0, tile, issue, 0, unroll=ROW_DMA_UNROLL)

    def drain(r, c):
        row_copy(0, buf0, 0).wait()
        row_copy(0, buf1, 0).wait()
        return c

    lax.fori_loop(0, tile, drain, 0, unroll=ROW_DMA_UNROLL)

    rt = rt_ref[0]
    f = rt[:, 2:3] * buf0[...] + rt[:, 3:4] * buf1[...]
    x = x_ref[0] + mod_ref[0][:, 5 * d:6 * d] * f
    if final:
        x = x * lax.rsqrt(jnp.mean(x * x, axis=-1, keepdims=True) + EPS) * gfin_ref[...]
    o_ref[0] = x


def _combine(dest, y, route, xc, modsel, g_final, n_ctx_tiles, tm, final):
    bsz, n, d = xc.shape
    tok = lambda w: pl.BlockSpec((1, tm, w), lambda b, i, dr: (b, i, 0))
    return pl.pallas_call(
        functools.partial(_combine_kernel, tile=tm, final=final),
        grid_spec=pltpu.PrefetchScalarGridSpec(
            num_scalar_prefetch=1,
            grid=(bsz, n // tm),
            in_specs=[pl.BlockSpec(memory_space=pl.ANY), tok(128), tok(d),
                      pl.BlockSpec((1, 1, modsel.shape[-1]),
                                   lambda b, i, dr: (2 * b + (i >= n_ctx_tiles).astype(jnp.int32), 0, 0)),
                      pl.BlockSpec((1, d), lambda b, i, dr: (0, 0))],
            out_specs=tok(d),
            scratch_shapes=[pltpu.VMEM((tm, d), F32), pltpu.VMEM((tm, d), F32),
                            pltpu.SemaphoreType.DMA(())]),
        out_shape=jax.ShapeDtypeStruct((bsz, n, d), F32),
        compiler_params=_cparams(("arbitrary", "arbitrary")),
        name="moe_combine",
    )(dest, y, route, xc, modsel, g_final)


def _routing_tables(route, counts, n_tok):
    e_flat = route[:, :TOP_K].astype(jnp.int32).reshape(n_tok * TOP_K)
    rank = route[:, 4:4 + TOP_K].astype(jnp.int32).reshape(n_tok * TOP_K)
    counts = counts[0, :N_EXPERTS].astype(jnp.int32)
    padded = (counts + MOE_BLOCK - 1) // MOE_BLOCK * MOE_BLOCK
    pad_end = jnp.cumsum(padded)
    pad_start = pad_end - padded
    dest = pad_start[e_flat] + rank
    n_blocks = (n_tok * TOP_K + N_EXPERTS * (MOE_BLOCK - 1) + MOE_BLOCK - 1) // MOE_BLOCK
    block_row = jnp.arange(n_blocks, dtype=jnp.int32) * MOE_BLOCK
    block_expert = jnp.minimum(
        jnp.sum((pad_end[None, :] <= block_row[:, None]).astype(jnp.int32), axis=1),
        N_EXPERTS - 1).astype(jnp.int32)
    first = jnp.concatenate([jnp.ones((1,), jnp.int32),
                             (block_expert[1:] != block_expert[:-1]).astype(jnp.int32)])
    n_used = (pad_end[-1] // MOE_BLOCK).astype(jnp.int32).reshape(1)
    return dest.astype(jnp.int32), block_expert, first, n_used, n_blocks


def kernel(x, c, ctx, c_ctx, w_ada, b_ada, g_mix, g_ffn, g_final, w_in, dn_conv, dn_a_log,
           dn_dt_bias, dn_norm_g, hg_lb_logits, hg_norm_g, w_br_dn, w_br_hg, w_out,
           w_router_grp, b_router_grp, w_router_exp, b_router_exp, w_exp_gate, w_exp_up,
           w_exp_down):
    bsz, n_lat, d = x.shape
    n_ctx = ctx.shape[1]
    depth = w_ada.shape[0]
    n = n_ctx + n_lat
    n_tok = bsz * n
    tm = 256
    n_ctx_tiles = n_ctx // tm
    hk = HEADS * DK

    lb_w = jax.nn.softmax(hg_lb_logits.astype(F32), axis=0)
    lower_bounds = jnp.cumsum(lb_w, axis=0) - lb_w[0]

    cc = jnp.zeros((16, d), F32).at[:bsz].set(c).at[bsz].set(c_ctx)
    mod_all = _modulation(cc, w_ada, b_ada)

    xc = jnp.concatenate([ctx, x], axis=1)

    o_qkv, o_z, o_b, o_a = 0, 3 * hk, 4 * hk, 4 * hk + 2 * HEADS
    o_hq = o_a + 2 * HEADS
    o_hf, o_hi, o_hg, o_gt = o_hq + hk, o_hq + 3 * hk, o_hq + 4 * hk, o_hq + 5 * hk

    for layer in range(depth):
        final = layer == depth - 1
        mod = mod_all[layer]
        modsel = jnp.stack([jnp.broadcast_to(mod[bsz], (bsz, 6 * d)), mod[:bsz]],
                           axis=1).reshape(2 * bsz, 1, 6 * d)
        wl = w_in[layer]
        w_main = jnp.concatenate([wl[:, o_gt:], wl[:, o_qkv:o_b], wl[:, o_hq:o_gt]],
                                 axis=1).astype(BF16)
        wb = wl[:, o_b:o_a].reshape(d, 2, HEADS)
        wa = wl[:, o_a:o_hq].reshape(d, 2, HEADS)
        w_small_t = jnp.concatenate([wb, wa, jnp.zeros((d, 4, HEADS), F32)], axis=1)
        w_small_t = w_small_t.transpose(2, 1, 0).reshape(8 * HEADS, d)
        pad4 = jnp.zeros((HEADS, 4, 1), F32)
        zero2 = jnp.zeros((HEADS, 2, 1), F32)
        alog = jnp.concatenate([zero2, dn_a_log[layer].T[:, :, None], pad4], axis=1)
        dtb = jnp.concatenate([zero2, dn_dt_bias[layer].T[:, :, None], pad4], axis=1)

        if layer == 0:
            proj, small = _projection(xc, modsel, g_mix[layer][None, :], w_main, w_small_t,
                                      n_ctx_tiles, tm)
        else:
            xc, proj, small = _combine_projection(
                dest, y, route, x_mid, modsel_prev, modsel, g_mix[layer][None, :], w_main,
                w_small_t, n_ctx_tiles, tm)
        odn = _deltanet(proj, small, dn_conv[layer], alog, dtb, n_ctx)

        ohg = _hgrn2(proj, lower_bounds[layer][None, :], n_ctx)

        wr = jnp.zeros((d, 128), F32).at[:, :N_GROUPS].set(w_router_grp[layer])
        wr = wr.at[:, N_GROUPS:N_GROUPS + N_EXPERTS].set(w_router_exp[layer])
        br = jnp.zeros((1, 128), F32).at[0, :N_GROUPS].set(b_router_grp[layer])
        br = br.at[0, N_GROUPS:N_GROUPS + N_EXPERTS].set(b_router_exp[layer])
        x_mid, h2, route, counts = _merge(
            proj, odn, ohg, xc, modsel, w_br_dn[layer].astype(BF16), w_br_hg[layer].astype(BF16),
            w_out[layer].astype(BF16), dn_norm_g[layer][None, :], hg_norm_g[layer][None, :],
            g_ffn[layer][None, :], wr, br, n_ctx_tiles, tm)

        dest, block_expert, first, n_used, n_blocks = _routing_tables(
            route.reshape(n_tok, 128), counts, n_tok)
        n_rows = n_blocks * MOE_BLOCK
        src = jnp.zeros((n_rows,), jnp.int32).at[dest].set(
            jnp.arange(n_tok * TOP_K, dtype=jnp.int32) // TOP_K, unique_indices=True)
        y = _experts(block_expert + layer * N_EXPERTS, first, n_used, src,
                     h2.reshape(n_tok, d), n_rows,
                     w_exp_gate.reshape(depth * N_EXPERTS, d, -1),
                     w_exp_up.reshape(depth * N_EXPERTS, d, -1),
                     w_exp_down.reshape(depth * N_EXPERTS, -1, d))
        modsel_prev = modsel

    out = _combine(dest, y, route, x_mid, modsel, g_final[None, :], n_ctx_tiles, tm, True)
    return out[:, n_ctx:]
```

```python
import functools

import jax
import jax.numpy as jnp
from jax import lax
from jax.experimental import pallas as pl
from jax.experimental.pallas import tpu as pltpu

F32 = jnp.float32
BF16 = jnp.bfloat16
HI = lax.Precision.HIGHEST

GRID_W = 64
CHUNK = 64
EPS = 1e-6
F_TINY = 1e-30
HEADS = 4
DK = 128
N_GROUPS = 4
EXPERTS_PER_GROUP = 8
N_EXPERTS = 32
TOP_K = 2
MOE_BLOCK = 512
SUB = 16
EXP_CLAMP = 60.0
DN_INTRA_UNROLL = 3
HG_STEP_UNROLL = 4
MERGE_COLS = 256
ROW_DMA_UNROLL = 8

C_GATES = 0
C_DNQKV = 2048
C_DNZ = 3584
C_HGQ = 4096
C_HGF = 4608
C_HGI = 5632
C_HGG = 6144
N_MAIN = 6656

VMEM_LIMIT = 56 * 1024 * 1024


def _cparams(sem):
    return pltpu.CompilerParams(dimension_semantics=sem, vmem_limit_bytes=VMEM_LIMIT)


def _silu(x):
    return x * jax.nn.sigmoid(x)


def _softplus(x):
    return jnp.maximum(x, 0.0) + jnp.log1p(jnp.exp(-jnp.abs(x)))


def _dot(a, b, prec=None):
    return jnp.dot(a, b, precision=prec, preferred_element_type=F32)


def _dot_nt(a, b, prec=None):
    return lax.dot_general(a, b, (((1,), (1,)), ((), ())), precision=prec,
                           preferred_element_type=F32)


def _dot_tn(a, b, prec=None):
    return lax.dot_general(a, b, (((0,), (0,)), ((), ())), precision=prec,
                           preferred_element_type=F32)


def _bdot(a, b):
    return _dot(a.astype(BF16), b.astype(BF16))


def _bdot_nt(a, b):
    return _dot_nt(a.astype(BF16), b.astype(BF16))


def _bdot_tn(a, b):
    return _dot_tn(a.astype(BF16), b.astype(BF16))


def _mod_kernel(cc_ref, w_ref, b_ref, o_ref):
    s = _silu(cc_ref[...])
    o_ref[0] = _dot(s, w_ref[0], HI) + b_ref[0]


def _modulation(cc, w_ada, b_ada):
    depth, d, n6 = w_ada.shape
    tn = n6 // 4
    return pl.pallas_call(
        _mod_kernel,
        grid=(depth, n6 // tn),
        in_specs=[pl.BlockSpec((16, d), lambda l, j: (0, 0)),
                  pl.BlockSpec((1, d, tn), lambda l, j: (l, 0, j)),
                  pl.BlockSpec((1, 1, tn), lambda l, j: (l, 0, j))],
        out_specs=pl.BlockSpec((1, 16, tn), lambda l, j: (l, 0, j)),
        out_shape=jax.ShapeDtypeStruct((depth, 16, n6), F32),
        compiler_params=_cparams(("parallel", "parallel")),
        name="modulation",
    )(cc, w_ada, b_ada.reshape(depth, 1, n6))


def _proj_kernel(x_ref, mod_ref, g_ref, w_ref, ws_ref, p_ref, s_ref):
    d = x_ref.shape[-1]
    x = x_ref[0]
    m = mod_ref[0]
    h = x * lax.rsqrt(jnp.mean(x * x, axis=-1, keepdims=True) + EPS) * g_ref[...]
    h = h * (1.0 + m[:, d:2 * d]) + m[:, :d]
    p_ref[0] = _dot(h.astype(BF16), w_ref[...])
    s_ref[0] = _dot_nt(ws_ref[...], h, HI)


def _combine_proj_kernel(dest_ref, y_ref, rt_ref, x_ref, modp_ref, mod_ref, g_ref, w_ref, ws_ref,
                         xo_ref, p_ref, s_ref, bufs, sems, *, tile):
    d = x_ref.shape[-1]
    steps_per_batch = pl.num_programs(1)
    step = pl.program_id(0) * steps_per_batch + pl.program_id(1)
    last = pl.num_programs(0) * steps_per_batch - 1
    slot = step % 2

    def row_copy(src, sl, k, r):
        return pltpu.make_async_copy(y_ref.at[pl.ds(src, 1)], bufs.at[sl, k, pl.ds(r, 1)],
                                     sems.at[sl])

    def issue_row(base, sl, r):
        t = base + r
        row_copy(dest_ref[TOP_K * t], sl, 0, r).start()
        row_copy(dest_ref[TOP_K * t + 1], sl, 1, r).start()

    def drain(sl):
        def wait_row(r, c):
            row_copy(0, sl, 0, 0).wait()
            row_copy(0, sl, 1, 0).wait()
            return c

        lax.fori_loop(0, tile, wait_row, 0, unroll=ROW_DMA_UNROLL)

    @pl.when(step == 0)
    def _():
        def first(r, c):
            issue_row(0, 0, r)
            return c

        lax.fori_loop(0, tile, first, 0, unroll=ROW_DMA_UNROLL)

    drain(slot)
    rt = rt_ref[0]
    f = rt[:, 2:3] * bufs[slot, 0] + rt[:, 3:4] * bufs[slot, 1]
    x = x_ref[0] + modp_ref[0][:, 5 * d:6 * d] * f
    xo_ref[0] = x

    next_base = jnp.minimum(step + 1, last) * tile
    for r in range(tile):
        issue_row(next_base, 1 - slot, r)

    m = mod_ref[0]
    h = x * lax.rsqrt(jnp.mean(x * x, axis=-1, keepdims=True) + EPS) * g_ref[...]
    h = h * (1.0 + m[:, d:2 * d]) + m[:, :d]
    p_ref[0] = _dot(h.astype(BF16), w_ref[...])
    s_ref[0] = _dot_nt(ws_ref[...], h, HI)

    @pl.when(step == last)
    def _():
        drain(1 - slot)


def _combine_projection(dest, y, route, x_mid, modsel_prev, modsel, g, w_main, w_small_t,
                        n_ctx_tiles, tm):
    bsz, n, d = x_mid.shape
    n_main = w_main.shape[1]
    tok = lambda w: pl.BlockSpec((1, tm, w), lambda b, i, dr: (b, i, 0))
    mod_spec = pl.BlockSpec(
        (1, 1, modsel.shape[-1]),
        lambda b, i, dr: (2 * b + (i >= n_ctx_tiles).astype(jnp.int32), 0, 0))
    return pl.pallas_call(
        functools.partial(_combine_proj_kernel, tile=tm),
        grid_spec=pltpu.PrefetchScalarGridSpec(
            num_scalar_prefetch=1,
            grid=(bsz, n // tm),
            in_specs=[pl.BlockSpec(memory_space=pl.ANY), tok(128), tok(d), mod_spec, mod_spec,
                      pl.BlockSpec((1, d), lambda b, i, dr: (0, 0)),
                      pl.BlockSpec((d, n_main), lambda b, i, dr: (0, 0),
                                   pipeline_mode=pl.Buffered(1)),
                      pl.BlockSpec((32, d), lambda b, i, dr: (0, 0))],
            out_specs=[tok(d), tok(n_main),
                       pl.BlockSpec((1, 32, tm), lambda b, i, dr: (b, 0, i))],
            scratch_shapes=[pltpu.VMEM((2, TOP_K, tm, d), F32), pltpu.SemaphoreType.DMA((2,))]),
        out_shape=[jax.ShapeDtypeStruct((bsz, n, d), F32),
                   jax.ShapeDtypeStruct((bsz, n, n_main), F32),
                   jax.ShapeDtypeStruct((bsz, 32, n), F32)],
        compiler_params=_cparams(("arbitrary", "arbitrary")),
        name="combine_projection",
    )(dest, y, route, x_mid, modsel_prev, modsel, g, w_main, w_small_t)


def _projection(xc, modsel, g, w_main, w_small_t, n_ctx_tiles, tm):
    bsz, n, d = xc.shape
    n_main = w_main.shape[1]
    return pl.pallas_call(
        _proj_kernel,
        grid=(bsz, n // tm),
        in_specs=[pl.BlockSpec((1, tm, d), lambda b, i: (b, i, 0)),
                  pl.BlockSpec((1, 1, modsel.shape[-1]),
                               lambda b, i: (2 * b + (i >= n_ctx_tiles).astype(jnp.int32), 0, 0)),
                  pl.BlockSpec((1, d), lambda b, i: (0, 0)),
                  pl.BlockSpec((d, n_main), lambda b, i: (0, 0), pipeline_mode=pl.Buffered(1)),
                  pl.BlockSpec((32, d), lambda b, i: (0, 0))],
        out_specs=[pl.BlockSpec((1, tm, n_main), lambda b, i: (b, i, 0)),
                   pl.BlockSpec((1, 32, tm), lambda b, i: (b, 0, i))],
        out_shape=[jax.ShapeDtypeStruct((bsz, n, n_main), F32),
                   jax.ShapeDtypeStruct((bsz, 32, n), F32)],
        compiler_params=_cparams(("parallel", "parallel")),
        name="projection",
    )(xc, modsel, g, w_main, w_small_t)


def _tri_masks(fwd):
    ii = lax.broadcasted_iota(jnp.int32, (CHUNK, CHUNK), 0)
    jj = lax.broadcasted_iota(jnp.int32, (CHUNK, CHUNK), 1)
    if fwd:
        return ii >= jj, ii > jj
    return ii <= jj, ii < jj


def _unit_tri_solve(a, rhs, mm):
    ii = lax.broadcasted_iota(jnp.int32, a[0].shape, 0)
    jj = lax.broadcasted_iota(jnp.int32, a[0].shape, 1)
    same = (ii // SUB) == (jj // SUB)
    eye = jnp.where(ii == jj, 1.0, 0.0)
    each = lambda f, *xs: [f(*t) for t in zip(*xs)]
    axpy = lambda x, y: each(lambda xi, yi: xi + mm(xi, yi), x, y)
    dm = each(lambda t: jnp.where(same, t, 0.0), a)
    lm = each(lambda t: jnp.where(same, 0.0, t), a)
    d2 = each(mm, dm, dm)
    d4 = each(mm, d2, d2)
    d8 = each(mm, d4, d4)
    p = axpy(each(lambda t: eye - t, dm), d2)
    p = axpy(p, d4)
    td = axpy(p, d8)
    m = each(mm, td, lm)
    m2 = each(mm, m, m)
    t1 = each(mm, td, rhs)
    t2 = each(lambda x, y: x + mm(y, x), t1, m2)
    return each(lambda x, y: x - mm(y, x), t2, m)


def _chunk_schedule(s, n_ctx_chunks, n_chunks):
    cf = s
    cb = jnp.where(s < n_ctx_chunks, n_ctx_chunks - 1 - s, n_chunks + n_ctx_chunks - 1 - s)
    return pl.multiple_of(cf * CHUNK, CHUNK), pl.multiple_of(cb * CHUNK, CHUNK)


def _seq_cumsum(x, fwd):
    n = x.shape[0]
    pos = lax.broadcasted_iota(jnp.int32, x.shape, 0) % CHUNK
    s = 1
    while s < CHUNK:
        if fwd:
            x = x + jnp.where(pos >= s, pltpu.roll(x, s, 0), 0.0)
        else:
            x = x + jnp.where(pos < CHUNK - s, pltpu.roll(x, n - s, 0), 0.0)
        s *= 2
    return x


def _dn_intra(q, k, v, beta, gc, fwds):
    m = q[0].shape[0]
    ii = lax.broadcasted_iota(jnp.int32, (m, m), 0)
    jj = lax.broadcasted_iota(jnp.int32, (m, m), 1)
    blk = ii // CHUNK
    same = blk == (jj // CHUNK)
    is_fwd = functools.reduce(lambda x, y: x | y,
                              [blk == g for g, f in enumerate(fwds) if f], blk < 0)
    incl = same & ((is_fwd & (ii >= jj)) | (~is_fwd & (ii <= jj)))
    strict = incl & (ii != jj)
    wide = lambda t: jnp.concatenate([t] * (m // DK), axis=1)
    tall = lambda t: jnp.concatenate([t] * (m // DK), axis=0)
    each = lambda f, *xs: [f(*t) for t in zip(*xs)]
    kk = each(_bdot_nt, k, k)
    qk = each(_bdot_nt, q, k)
    dec = each(lambda g: jnp.where(incl, jnp.exp(jnp.where(incl, wide(g) - tall(g.T), 0.0)), 0.0), gc)
    a = each(lambda x, b, d: jnp.where(strict, x * wide(b) * d, 0.0), kk, beta, dec)
    eg = each(jnp.exp, gc)
    rhs = each(lambda vi, ki, b, e: jnp.concatenate([vi * b, ki * (b * e)], axis=1), v, k, beta, eg)
    sol = _unit_tri_solve(a, rhs, _bdot)

    def to_end(g):
        rows = [g[i * CHUNK + (CHUNK - 1 if f else 0)][None, :] for i, f in enumerate(fwds)]
        return jnp.concatenate([jnp.broadcast_to(r, (CHUNK, DK)) for r in rows], axis=0) - g

    n_stack = len(fwds)
    ri = lax.broadcasted_iota(jnp.int32, (n_stack * DK, m), 0) // DK
    ci = lax.broadcasted_iota(jnp.int32, (n_stack * DK, m), 1) // CHUNK
    kdt_bd = each(lambda ki, g: jnp.where(
        ri == ci, jnp.concatenate([(ki * jnp.exp(to_end(g))).T] * n_stack, axis=0), 0.0), k, gc)
    att = each(lambda x, d: jnp.where(incl, x * d, 0.0), qk, dec)
    att_uw = each(_bdot, att, sol)
    kd_uw = each(_bdot, kdt_bd, sol)
    q_eff = each(lambda x, e, aw: x * e - aw[:, DK:], q, eg, att_uw)
    decay = each(lambda g: [jnp.exp(g[i * CHUNK + (CHUNK - 1 if f else 0)][None, :])
                            for i, f in enumerate(fwds)], gc)
    return [(aw[:, :DK], qe, kw, dc) for aw, qe, kw, dc in zip(att_uw, q_eff, kd_uw, decay)]


DN_PAIR_ORDER = ((0, 2), (3, 1))


def _dn_compose(stacks):
    blk = lambda t, g, size: t[g * size:(g + 1) * size]
    lhs, rhs = [], []
    for o0, qe, kw, dc in stacks:
        for first, second in DN_PAIR_ORDER:
            lhs.append(jnp.concatenate([blk(kw, second, DK)[:, DK:], blk(qe, second, CHUNK)], axis=0))
            rhs.append(blk(kw, first, DK))
    z = [_bdot(a, b) for a, b in zip(lhs, rhs)]
    out = []
    for s, (o0, qe, kw, dc) in enumerate(stacks):
        o0b = [blk(o0, g, CHUNK) for g in range(4)]
        qb = [blk(qe, g, CHUNK) for g in range(4)]
        pairs = []
        for di, (first, second) in enumerate(DN_PAIR_ORDER):
            zi = z[2 * s + di]
            k1, k2 = blk(kw, first, DK), blk(kw, second, DK)
            e1, e2 = dc[first], dc[second]
            c12 = e2 * k1[:, :DK] + k2[:, :DK] - zi[:DK, :DK]
            n12 = e2 * k1[:, DK:] + e1 * k2[:, DK:] - zi[:DK, DK:]
            o0b[second] = o0b[second] + zi[DK:, :DK]
            qb[second] = e1 * qb[second] - zi[DK:, DK:]
            pairs.append((c12, n12, e1 * e2))
        out.append((o0b, qb, pairs))
    return out


def _dn_kernel(q_ref, k_ref, v_ref, cq_ref, ck_ref, cv_ref, sm_ref, alog_ref, dtb_ref,
               o_ref, qs, ks, vs, bfs, bbs, gfs, gbs, qes, cs, ns, es, *, n_ctx):
    n = q_ref.shape[1]
    row = lax.broadcasted_iota(jnp.int32, (n, DK), 0)
    seg_first = (row == 0) | (row == n_ctx)
    seg_last = (row == n_ctx - 1) | (row == n - 1)

    def conv_act(x_ref, c_ref):
        x = x_ref[0]
        w = c_ref[...]
        xm = jnp.where(seg_first, 0.0, pltpu.roll(x, 1, 0))
        xp = jnp.where(seg_last, 0.0, pltpu.roll(x, n - 1, 0))
        return _silu(xm * w[0:1] + x * w[1:2] + xp * w[2:3])

    def l2n(y):
        return y * lax.rsqrt(jnp.sum(y * y, axis=-1, keepdims=True) + EPS)

    qs[...] = l2n(conv_act(q_ref, cq_ref)) * (DK ** -0.5)
    ks[...] = l2n(conv_act(k_ref, ck_ref))
    vs[...] = conv_act(v_ref, cv_ref)

    sm = sm_ref[0]
    beta = jax.nn.sigmoid(sm)
    g = -jnp.exp(alog_ref[0]) * _softplus(sm + dtb_ref[0])
    pos = lax.broadcasted_iota(jnp.int32, sm.shape, 1) % CHUNK
    pre, suf = g, g
    s = 1
    while s < CHUNK:
        pre = pre + jnp.where(pos >= s, pltpu.roll(pre, s, 1), 0.0)
        suf = suf + jnp.where(pos < CHUNK - s, pltpu.roll(suf, n - s, 1), 0.0)
        s *= 2

    def col(r):
        return jnp.broadcast_to(r, (DK, n)).T

    bfs[...] = col(beta[0:1])
    bbs[...] = col(beta[1:2])
    gfs[...] = col(pre[2:3])
    gbs[...] = col(suf[3:4])

    n_chunks = n // CHUNK
    n_ctx_chunks = n_ctx // CHUNK

    def intra(c, carry):
        starts = [pl.multiple_of((c * DN_INTRA_UNROLL + j) * 2 * CHUNK, 2 * CHUNK)
                  for j in range(DN_INTRA_UNROLL)]
        pairs = [pl.ds(r0, 2 * CHUNK) for r0 in starts]
        stack = lambda t: jnp.concatenate([t[:CHUNK], t[:CHUNK], t[CHUNK:], t[CHUNK:]], axis=0)
        both = lambda tf, tb: jnp.concatenate(
            [tf[:CHUNK], tb[:CHUNK], tf[CHUNK:], tb[CHUNK:]], axis=0)
        results = _dn_intra(
            [stack(qs[p, :]) for p in pairs], [stack(ks[p, :]) for p in pairs],
            [stack(vs[p, :]) for p in pairs],
            [both(bfs[p, :], bbs[p, :]) for p in pairs],
            [both(gfs[p, :], gbs[p, :]) for p in pairs], (True, False, True, False))
        for r0, (o0, q_eff, pair_maps) in zip(starts, _dn_compose(results)):
            pair = pl.ds(r0, 2 * CHUNK)
            o_ref[0, pair, :] = jnp.concatenate([o0[0] + o0[1], o0[2] + o0[3]], axis=0)
            for di, (c12, n12, e12) in enumerate(pair_maps):
                qes[di, pair, :] = jnp.concatenate([q_eff[di], q_eff[2 + di]], axis=0).astype(BF16)
                cs[di, pair, :] = c12
                ns[di, pair, :] = n12.astype(BF16)
                es[di, pl.ds(r0 // (2 * CHUNK), 1), :] = e12
        return carry

    lax.fori_loop(0, n_chunks // (2 * DN_INTRA_UNROLL), intra, 0)

    def step(s, states):
        first_b = jnp.where(2 * s < n_ctx_chunks, n_ctx_chunks - 1 - 2 * s,
                            n_chunks + n_ctx_chunks - 1 - 2 * s)
        pair_idx = (s, first_b // 2)
        pairs = [pl.ds(pl.multiple_of(p * 2 * CHUNK, 2 * CHUNK), 2 * CHUNK) for p in pair_idx]
        stb = [st.astype(BF16) for st in states]
        corr = [_dot(ns[di, pairs[di], :], stb[di]) for di in range(2)]
        outs = [_dot(qes[di, pairs[di], :], stb[di]) for di in range(2)]
        new_states = []
        for di in range(2):
            decay = es[di, pl.ds(pair_idx[di], 1), :]
            new_states.append(states[di] * decay + cs[di, pairs[di], :] - corr[di])
            o_ref[0, pairs[di], :] = o_ref[0, pairs[di], :] + outs[di]
        return tuple(new_states)

    zero = jnp.zeros((DK, DK), F32)
    lax.fori_loop(0, n_chunks // 2, step, (zero, zero))


def _deltanet(proj, small, conv_w, alog, dtb, n_ctx):
    bsz, n, _ = proj.shape
    qb = C_DNQKV // DK
    seq = lambda off: pl.BlockSpec((1, n, DK), lambda b, h: (b, 0, off + h))
    cw = lambda off: pl.BlockSpec((3, DK), lambda b, h: (0, off + h))
    return pl.pallas_call(
        functools.partial(_dn_kernel, n_ctx=n_ctx),
        grid=(bsz, HEADS),
        in_specs=[seq(qb), seq(qb + HEADS), seq(qb + 2 * HEADS),
                  cw(0), cw(HEADS), cw(2 * HEADS),
                  pl.BlockSpec((1, 8, n), lambda b, h: (b, h, 0)),
                  pl.BlockSpec((1, 8, 1), lambda b, h: (h, 0, 0)),
                  pl.BlockSpec((1, 8, 1), lambda b, h: (h, 0, 0))],
        out_specs=pl.BlockSpec((1, n, DK), lambda b, h: (b, 0, h)),
        out_shape=jax.ShapeDtypeStruct((bsz, n, HEADS * DK), F32),
        scratch_shapes=[pltpu.VMEM((n, DK), F32) for _ in range(7)] + [
            pltpu.VMEM((2, n, DK), BF16),
            pltpu.VMEM((2, n, DK), F32),
            pltpu.VMEM((2, n, DK), BF16),
            pltpu.VMEM((2, -(-n // (16 * CHUNK)) * 8, DK), F32)],
        compiler_params=_cparams(("parallel", "parallel")),
        name="deltanet",
    )(proj, proj, proj, conv_w, conv_w, conv_w, small, alog, dtb)


def _hg_intra(q, zf, zb, v, lb):
    fwds = (True, False, True, False)
    n_stack = len(fwds)
    nsub = CHUNK // SUB

    def gate(z):
        f = lb + (1.0 - lb) * jax.nn.sigmoid(z)
        return (1.0 - lb) * jax.nn.sigmoid(-z), jnp.log(jnp.maximum(f, F_TINY))

    kf_f, lf_f = gate(zf)
    kf_b, lf_b = gate(zb)
    both = lambda tf, tb: jnp.concatenate([tf[:CHUNK], tb[:CHUNK], tf[CHUNK:], tb[CHUNK:]], axis=0)
    stack = lambda t: both(t, t)
    q_st = stack(_silu(q))
    v_st = stack(v)
    kf = both(kf_f, kf_b)
    gc = both(_seq_cumsum(lf_f, True), _seq_cumsum(lf_b, False))

    m = n_stack * CHUNK
    ii = lax.broadcasted_iota(jnp.int32, (m, m), 0)
    jj = lax.broadcasted_iota(jnp.int32, (m, m), 1)
    blk = ii // CHUNK
    is_fwd = functools.reduce(lambda x, y: x | y,
                              [blk == g for g, f in enumerate(fwds) if f], blk < 0)
    incl = (blk == (jj // CHUNK)) & ((is_fwd & (ii >= jj)) | (~is_fwd & (ii <= jj)))

    def ref_row(g, i):
        r = g * CHUNK + i * SUB + (0 if fwds[g] else SUB - 1)
        return gc[r:r + 1, :]

    rep = lambda r, k: jnp.broadcast_to(r, (k, DK))
    qe = q_st * jnp.exp(gc - jnp.concatenate(
        [rep(ref_row(g, i), SUB) for g in range(n_stack) for i in range(nsub)], axis=0))
    scores = []
    for i in range(nsub):
        ref = jnp.concatenate([rep(ref_row(g, i), CHUNK) for g in range(n_stack)], axis=0)
        ke = kf * jnp.exp(jnp.minimum(ref - gc, EXP_CLAMP))
        lhs = jnp.concatenate([qe[g * CHUNK + i * SUB:g * CHUNK + (i + 1) * SUB]
                               for g in range(n_stack)], axis=0)
        scores.append(_bdot_nt(lhs, ke))
    att = jnp.concatenate([scores[i][g * SUB:(g + 1) * SUB]
                           for g in range(n_stack) for i in range(nsub)], axis=0)
    o0 = _bdot(jnp.where(incl, att, 0.0), v_st)

    end_rows = [gc[g * CHUNK + (CHUNK - 1 if f else 0)][None, :] for g, f in enumerate(fwds)]
    kd = kf * jnp.exp(jnp.concatenate([rep(r, CHUNK) for r in end_rows], axis=0) - gc)
    ri = lax.broadcasted_iota(jnp.int32, (n_stack * DK, m), 0) // DK
    ci = lax.broadcasted_iota(jnp.int32, (n_stack * DK, m), 1) // CHUNK
    vt_bd = jnp.where(ri == ci, jnp.concatenate([v_st.T] * n_stack, axis=0), 0.0)
    ct = _bdot(vt_bd, kd)
    return o0, q_st * jnp.exp(gc), ct, [jnp.exp(r) for r in end_rows]


def _hg_kernel(q_ref, zf_ref, zb_ref, v_ref, lb_ref, o_ref, oscan, qgs, cts, es, *, n_ctx):
    n = q_ref.shape[1]
    n_lat = n - n_ctx
    col_len = n_lat // GRID_W
    cols_per_pair = 2 * CHUNK // col_len
    lb = lb_ref[...]

    def store(r0, res):
        o0, qg, ct, decay = res
        for g in range(4):
            di = g % 2
            rg = r0 + (g // 2) * CHUNK
            sl = pl.ds(rg, CHUNK)
            if di == 0:
                oscan[sl, :] = o0[g * CHUNK:(g + 1) * CHUNK] + o0[(g + 1) * CHUNK:(g + 2) * CHUNK]
            qgs[di, sl, :] = qg[g * CHUNK:(g + 1) * CHUNK].astype(BF16)
            cts[di, pl.ds(pl.multiple_of(2 * rg, 2 * CHUNK), DK), :] = ct[g * DK:(g + 1) * DK]
            es[di, pl.ds(rg // CHUNK, 1), :] = decay[g]

    def ctx_pair(i, c):
        r0 = pl.multiple_of(i * 2 * CHUNK, 2 * CHUNK)
        sl = pl.ds(r0, 2 * CHUNK)
        store(r0, _hg_intra(q_ref[0, sl, :], zf_ref[0, sl, :], zb_ref[0, sl, :], v_ref[0, sl, :], lb))
        return c

    lax.fori_loop(0, n_ctx // (2 * CHUNK), ctx_pair, 0)

    def lat_pair(i, c):
        def load(ref):
            return jnp.concatenate(
                [ref[0, pl.ds(n_ctx + cols_per_pair * i + j, col_len, stride=GRID_W), :]
                 for j in range(cols_per_pair)], axis=0)

        r0 = pl.multiple_of(n_ctx + i * 2 * CHUNK, 2 * CHUNK)
        store(r0, _hg_intra(load(q_ref), load(zf_ref), load(zb_ref), load(v_ref), lb))
        return c

    lax.fori_loop(0, n_lat // (2 * CHUNK), lat_pair, 0)

    n_chunks = n // CHUNK
    n_ctx_chunks = n_ctx // CHUNK

    def steps(t, states):
        for j in range(HG_STEP_UNROLL):
            rows = _chunk_schedule(t * HG_STEP_UNROLL + j, n_ctx_chunks, n_chunks)
            sls = [pl.ds(r0, CHUNK) for r0 in rows]
            outs = [_dot_nt(qgs[di, sls[di], :], states[di].astype(BF16)) for di in range(2)]
            new_states = []
            for di in range(2):
                decay = es[di, pl.ds(rows[di] // CHUNK, 1), :]
                c = cts[di, pl.ds(pl.multiple_of(2 * rows[di], 2 * CHUNK), DK), :]
                new_states.append(states[di] * decay + c)
                oscan[sls[di], :] = oscan[sls[di], :] + outs[di]
            states = tuple(new_states)
        return states

    zero = jnp.zeros((DK, DK), F32)
    lax.fori_loop(0, n_chunks // HG_STEP_UNROLL, steps, (zero, zero))

    o_ref[0, pl.ds(0, n_ctx), :] = oscan[pl.ds(0, n_ctx), :]

    def grid_row(r, c):
        dst = pl.ds(pl.multiple_of(n_ctx + r * GRID_W, GRID_W), GRID_W)
        o_ref[0, dst, :] = oscan[pl.ds(n_ctx + r, GRID_W, stride=col_len), :]
        return c

    lax.fori_loop(0, col_len, grid_row, 0)


def _hgrn2(proj, lb, n_ctx):
    bsz, n, _ = proj.shape
    seq = lambda off: pl.BlockSpec((1, n, DK), lambda b, h: (b, 0, off // DK + h))
    n_chunks = n // CHUNK
    return pl.pallas_call(
        functools.partial(_hg_kernel, n_ctx=n_ctx),
        grid=(bsz, HEADS),
        in_specs=[seq(C_HGQ), seq(C_HGF), seq(C_HGF + HEADS * DK), seq(C_HGI),
                  pl.BlockSpec((1, DK), lambda b, h: (0, h))],
        out_specs=pl.BlockSpec((1, n, DK), lambda b, h: (b, 0, h)),
        out_shape=jax.ShapeDtypeStruct((bsz, n, HEADS * DK), F32),
        scratch_shapes=[pltpu.VMEM((n, DK), F32),
                        pltpu.VMEM((2, n, DK), BF16),
                        pltpu.VMEM((2, 2 * n, DK), F32),
                        pltpu.VMEM((2, -(-n_chunks // 8) * 8, DK), F32)],
        compiler_params=_cparams(("parallel", "parallel")),
        name="hgrn2",
    )(proj, proj, proj, proj, lb)


def _gated_rmsnorm(o, z, g):
    parts = []
    for h in range(HEADS):
        oh = o[:, h * DK:(h + 1) * DK]
        zh = z[:, h * DK:(h + 1) * DK]
        y = oh * lax.rsqrt(jnp.mean(oh * oh, axis=-1, keepdims=True) + EPS) * g
        parts.append(y * _silu(zh))
    return jnp.concatenate(parts, axis=1)


def _route(lg):
    lane = lax.broadcasted_iota(jnp.int32, lg.shape, 1).astype(F32)
    neg = -1e30
    big = 1e9
    is_grp = lane < N_GROUPS
    gl = jnp.where(is_grp, lg, neg)
    gmax = jnp.max(gl, axis=-1, keepdims=True)
    gsel = jnp.min(jnp.where(gl == gmax, lane, big), axis=-1, keepdims=True)
    gp = 1.0 / jnp.sum(jnp.where(is_grp, jnp.exp(gl - gmax), 0.0), axis=-1, keepdims=True)
    lo = N_GROUPS + EXPERTS_PER_GROUP * gsel
    el = jnp.where((lane >= lo) & (lane < lo + EXPERTS_PER_GROUP), lg, neg)
    m1 = jnp.max(el, axis=-1, keepdims=True)
    i1 = jnp.min(jnp.where(el == m1, lane, big), axis=-1, keepdims=True)
    el2 = jnp.where(lane == i1, neg, el)
    m2 = jnp.max(el2, axis=-1, keepdims=True)
    i2 = jnp.min(jnp.where(el2 == m2, lane, big), axis=-1, keepdims=True)
    r = jnp.exp(m2 - m1)
    g1 = gp / (1.0 + r)
    g2 = g1 * r
    out = jnp.where(lane == 0, i1 - N_GROUPS, 0.0)
    out = jnp.where(lane == 1, i2 - N_GROUPS, out)
    out = jnp.where(lane == 2, g1, out)
    return jnp.where(lane == 3, g2, out)


def _merge_kernel(gate_ref, z_ref, hg_ref, odn_ref, ohg_ref, x_ref, mod_ref, wdn_ref, whg_ref,
                  wout_ref, gdn_ref, ghg_ref, gffn_ref, wrh_ref, wrl_ref, br_ref,
                  xo_ref, h2_ref, rt_ref, cnt_ref, mix_s):
    d = x_ref.shape[-1]
    cols = [slice(j * MERGE_COLS, (j + 1) * MERGE_COLS) for j in range(d // MERGE_COLS)]
    mod = lambda k, cs: mod_ref[0, :, k * d + cs.start:k * d + cs.stop]

    @pl.when((pl.program_id(0) == 0) & (pl.program_id(1) == 0))
    def _():
        cnt_ref[...] = jnp.zeros(cnt_ref.shape, F32)

    a_dn = _gated_rmsnorm(odn_ref[0], z_ref[0], gdn_ref[...]).astype(BF16)
    a_hg = _gated_rmsnorm(ohg_ref[0], hg_ref[0], ghg_ref[...]).astype(BF16)
    for cs in cols:
        br_dn = _dot(a_dn, wdn_ref[:, cs])
        br_hg = _dot(a_hg, whg_ref[:, cs])
        g_dn = jax.nn.sigmoid(gate_ref[0, :, cs])
        g_hg = jax.nn.sigmoid(gate_ref[0, :, d + cs.start:d + cs.stop])
        mix_s[:, cs] = (g_dn * br_dn + g_hg * br_hg).astype(BF16)
    mix = mix_s[...]
    ssq = jnp.zeros((x_ref.shape[1], 1), F32)
    for cs in cols:
        x = x_ref[0, :, cs] + mod(2, cs) * _dot(mix, wout_ref[:, cs])
        xo_ref[0, :, cs] = x
        ssq = ssq + jnp.sum(x * x, axis=-1, keepdims=True)
    scale = lax.rsqrt(ssq * (1.0 / d) + EPS)
    lg = jnp.zeros(rt_ref.shape[1:], F32) + br_ref[...]
    for cs in cols:
        h2 = xo_ref[0, :, cs] * scale * gffn_ref[:, cs] * (1.0 + mod(4, cs)) + mod(3, cs)
        h2_ref[0, :, cs] = h2
        h_hi = h2.astype(BF16)
        h_lo = (h2 - h_hi.astype(F32)).astype(BF16)
        lg = lg + (_dot(h_hi, wrh_ref[cs, :]) + (_dot(h_hi, wrl_ref[cs, :]) + _dot(h_lo, wrh_ref[cs, :])))
    rt = _route(lg)
    tm = rt.shape[0]
    lane = lax.broadcasted_iota(jnp.int32, rt.shape, 1)
    lanef = lane.astype(F32)
    oh1 = jnp.where(lanef == rt[:, 0:1], 1.0, 0.0)
    oh2 = jnp.where(lanef == rt[:, 1:2], 1.0, 0.0)
    both = oh1 + oh2
    ii = lax.broadcasted_iota(jnp.int32, (tm, tm), 0)
    jj = lax.broadcasted_iota(jnp.int32, (tm, tm), 1)
    earlier = jnp.where(ii > jj, 1.0, 0.0).astype(BF16)
    prior = _dot(earlier, both.astype(BF16)) + cnt_ref[0:1, :]
    rank1 = jnp.sum(prior * oh1, axis=-1, keepdims=True)
    rank2 = jnp.sum(prior * oh2, axis=-1, keepdims=True)
    rt = jnp.where(lane == 4, rank1, rt)
    rt_ref[0] = jnp.where(lane == 5, rank2, rt)
    cnt_ref[0:1, :] = cnt_ref[0:1, :] + jnp.sum(both, axis=0, keepdims=True)


def _merge(proj, odn, ohg, xc, modsel, wdn, whg, wout, gdn, ghg, gffn, wr, br, n_ctx_tiles, tm):
    wr_hi = wr.astype(BF16)
    wr_lo = (wr - wr_hi.astype(F32)).astype(BF16)
    bsz, n, d = xc.shape
    hv = HEADS * DK
    tok = lambda w, j: pl.BlockSpec((1, tm, w), lambda b, i: (b, i, j))
    full = lambda a: pl.BlockSpec(a.shape, lambda b, i: (0,) * a.ndim)
    return pl.pallas_call(
        _merge_kernel,
        grid=(bsz, n // tm),
        in_specs=[tok(2 * d, C_GATES // (2 * d)), tok(hv, C_DNZ // hv), tok(hv, C_HGG // hv),
                  tok(hv, 0), tok(hv, 0), tok(d, 0),
                  pl.BlockSpec((1, 1, modsel.shape[-1]),
                               lambda b, i: (2 * b + (i >= n_ctx_tiles).astype(jnp.int32), 0, 0)),
                  full(wdn), full(whg), full(wout), full(gdn), full(ghg), full(gffn),
                  full(wr_hi), full(wr_lo), full(br)],
        out_specs=[tok(d, 0), tok(d, 0), tok(128, 0),
                   pl.BlockSpec((8, 128), lambda b, i: (0, 0))],
        out_shape=[jax.ShapeDtypeStruct((bsz, n, d), F32),
                   jax.ShapeDtypeStruct((bsz, n, d), F32),
                   jax.ShapeDtypeStruct((bsz, n, 128), F32),
                   jax.ShapeDtypeStruct((8, 128), F32)],
        scratch_shapes=[pltpu.VMEM((tm, d), BF16)],
        compiler_params=_cparams(("arbitrary", "arbitrary")),
        name="merge",
    )(proj, proj, proj, odn, ohg, xc, modsel, wdn, whg, wout, gdn, ghg, gffn, wr_hi, wr_lo, br)


def _dispatch_kernel(dest_ref, h_ref, init_ref, rows_ref, sem, *, tile):
    del init_ref
    base = pl.program_id(0) * tile

    def row_copy(r, d):
        return pltpu.make_async_copy(h_ref.at[pl.ds(r, 1)], rows_ref.at[pl.ds(d, 1)], sem)

    def issue(r, c):
        t = base + r
        row_copy(r, dest_ref[TOP_K * t]).start()
        row_copy(r, dest_ref[TOP_K * t + 1]).start()
        return c

    lax.fori_loop(0, tile, issue, 0, unroll=ROW_DMA_UNROLL)

    def drain(r, c):
        row_copy(0, 0).wait()
        row_copy(0, 0).wait()
        return c

    lax.fori_loop(0, tile, drain, 0, unroll=ROW_DMA_UNROLL)


def _dispatch(dest, h2, n_rows, tile):
    n_tok, d = h2.shape
    return pl.pallas_call(
        functools.partial(_dispatch_kernel, tile=tile),
        grid_spec=pltpu.PrefetchScalarGridSpec(
            num_scalar_prefetch=1,
            grid=(n_tok // tile,),
            in_specs=[pl.BlockSpec((tile, d), lambda i, dr: (i, 0)),
                      pl.BlockSpec(memory_space=pl.ANY)],
            out_specs=pl.BlockSpec(memory_space=pl.ANY),
            scratch_shapes=[pltpu.SemaphoreType.DMA(())]),
        out_shape=jax.ShapeDtypeStruct((n_rows, d), F32),
        input_output_aliases={2: 0},
        compiler_params=pltpu.CompilerParams(dimension_semantics=("arbitrary",),
                                             has_side_effects=True),
        name="moe_dispatch",
    )(dest, h2, jnp.zeros((n_rows, d), F32))


def _expert_kernel(be_ref, first_ref, nused_ref, x_ref, wg_ref, wu_ref, wd_ref, o_ref,
                   wg_s, wu_s, wd_s):
    i = pl.program_id(0)

    @pl.when(first_ref[i] == 1)
    def _():
        wg_s[...] = wg_ref[0].astype(BF16)
        wu_s[...] = wu_ref[0].astype(BF16)
        wd_s[...] = wd_ref[0].astype(BF16)

    @pl.when(i < nused_ref[0])
    def _():
        x = x_ref[...].astype(BF16)
        a = _silu(_dot(x, wg_s[...])) * _dot(x, wu_s[...])
        o_ref[...] = _dot(a.astype(BF16), wd_s[...])

    @pl.when(i >= nused_ref[0])
    def _():
        o_ref[...] = jnp.zeros(o_ref.shape, F32)


def _experts(block_expert, first, n_used, rows, wg, wu, wd):
    n_rows, d = rows.shape
    de = wg.shape[-1]
    return pl.pallas_call(
        _expert_kernel,
        grid_spec=pltpu.PrefetchScalarGridSpec(
            num_scalar_prefetch=3,
            grid=(n_rows // MOE_BLOCK,),
            in_specs=[pl.BlockSpec((MOE_BLOCK, d), lambda i, be, f, nu: (jnp.minimum(i, nu[0] - 1), 0)),
                      pl.BlockSpec((1, d, de), lambda i, be, f, nu: (be[i], 0, 0)),
                      pl.BlockSpec((1, d, de), lambda i, be, f, nu: (be[i], 0, 0)),
                      pl.BlockSpec((1, de, d), lambda i, be, f, nu: (be[i], 0, 0))],
            out_specs=pl.BlockSpec((MOE_BLOCK, d), lambda i, be, f, nu: (i, 0)),
            scratch_shapes=[pltpu.VMEM((d, de), BF16), pltpu.VMEM((d, de), BF16),
                            pltpu.VMEM((de, d), BF16)]),
        out_shape=jax.ShapeDtypeStruct((n_rows, d), F32),
        compiler_params=_cparams(("arbitrary",)),
        name="moe_experts",
    )(block_expert, first, n_used, rows, wg, wu, wd)


def _combine_kernel(dest_ref, y_ref, rt_ref, x_ref, mod_ref, gfin_ref, o_ref, buf0, buf1, sem,
                    *, tile, final):
    d = x_ref.shape[-1]
    base = (pl.program_id(0) * pl.num_programs(1) + pl.program_id(1)) * tile

    def row_copy(src, buf, r):
        return pltpu.make_async_copy(y_ref.at[pl.ds(src, 1)], buf.at[pl.ds(r, 1)], sem)

    def issue(r, c):
        t = base + r
        row_copy(dest_ref[TOP_K * t], buf0, r).start()
        row_copy(dest_ref[TOP_K * t + 1], buf1, r).start()
        return c

    lax.fori_loop(0, tile, issue, 0, unroll=ROW_DMA_UNROLL)

    def drain(r, c):
        row_copy(0, buf0, 0).wait()
        row_copy(0, buf1, 0).wait()
        return c

    lax.fori_loop(0, tile, drain, 0, unroll=ROW_DMA_UNROLL)

    rt = rt_ref[0]
    f = rt[:, 2:3] * buf0[...] + rt[:, 3:4] * buf1[...]
    x = x_ref[0] + mod_ref[0][:, 5 * d:6 * d] * f
    if final:
        x = x * lax.rsqrt(jnp.mean(x * x, axis=-1, keepdims=True) + EPS) * gfin_ref[...]
    o_ref[0] = x


def _combine(dest, y, route, xc, modsel, g_final, n_ctx_tiles, tm, final):
    bsz, n, d = xc.shape
    tok = lambda w: pl.BlockSpec((1, tm, w), lambda b, i, dr: (b, i, 0))
    return pl.pallas_call(
        functools.partial(_combine_kernel, tile=tm, final=final),
        grid_spec=pltpu.PrefetchScalarGridSpec(
            num_scalar_prefetch=1,
            grid=(bsz, n // tm),
            in_specs=[pl.BlockSpec(memory_space=pl.ANY), tok(128), tok(d),
                      pl.BlockSpec((1, 1, modsel.shape[-1]),
                                   lambda b, i, dr: (2 * b + (i >= n_ctx_tiles).astype(jnp.int32), 0, 0)),
                      pl.BlockSpec((1, d), lambda b, i, dr: (0, 0))],
            out_specs=tok(d),
            scratch_shapes=[pltpu.VMEM((tm, d), F32), pltpu.VMEM((tm, d), F32),
                            pltpu.SemaphoreType.DMA(())]),
        out_shape=jax.ShapeDtypeStruct((bsz, n, d), F32),
        compiler_params=_cparams(("arbitrary", "arbitrary")),
        name="moe_combine",
    )(dest, y, route, xc, modsel, g_final)


def _routing_tables(route, counts, n_tok):
    e_flat = route[:, :TOP_K].astype(jnp.int32).reshape(n_tok * TOP_K)
    rank = route[:, 4:4 + TOP_K].astype(jnp.int32).reshape(n_tok * TOP_K)
    counts = counts[0, :N_EXPERTS].astype(jnp.int32)
    padded = (counts + MOE_BLOCK - 1) // MOE_BLOCK * MOE_BLOCK
    pad_end = jnp.cumsum(padded)
    pad_start = pad_end - padded
    dest = pad_start[e_flat] + rank
    n_blocks = (n_tok * TOP_K + N_EXPERTS * (MOE_BLOCK - 1) + MOE_BLOCK - 1) // MOE_BLOCK
    block_row = jnp.arange(n_blocks, dtype=jnp.int32) * MOE_BLOCK
    block_expert = jnp.minimum(
        jnp.sum((pad_end[None, :] <= block_row[:, None]).astype(jnp.int32), axis=1),
        N_EXPERTS - 1).astype(jnp.int32)
    first = jnp.concatenate([jnp.ones((1,), jnp.int32),
                             (block_expert[1:] != block_expert[:-1]).astype(jnp.int32)])
    n_used = (pad_end[-1] // MOE_BLOCK).astype(jnp.int32).reshape(1)
    return dest.astype(jnp.int32), block_expert, first, n_used, n_blocks


def kernel(x, c, ctx, c_ctx, w_ada, b_ada, g_mix, g_ffn, g_final, w_in, dn_conv, dn_a_log,
           dn_dt_bias, dn_norm_g, hg_lb_logits, hg_norm_g, w_br_dn, w_br_hg, w_out,
           w_router_grp, b_router_grp, w_router_exp, b_router_exp, w_exp_gate, w_exp_up,
           w_exp_down):
    bsz, n_lat, d = x.shape
    n_ctx = ctx.shape[1]
    depth = w_ada.shape[0]
    n = n_ctx + n_lat
    n_tok = bsz * n
    tm = 256
    n_ctx_tiles = n_ctx // tm
    hk = HEADS * DK

    lb_w = jax.nn.softmax(hg_lb_logits.astype(F32), axis=0)
    lower_bounds = jnp.cumsum(lb_w, axis=0) - lb_w[0]

    cc = jnp.zeros((16, d), F32).at[:bsz].set(c).at[bsz].set(c_ctx)
    mod_all = _modulation(cc, w_ada, b_ada)

    xc = jnp.concatenate([ctx, x], axis=1)

    o_qkv, o_z, o_b, o_a = 0, 3 * hk, 4 * hk, 4 * hk + 2 * HEADS
    o_hq = o_a + 2 * HEADS
    o_hf, o_hi, o_hg, o_gt = o_hq + hk, o_hq + 3 * hk, o_hq + 4 * hk, o_hq + 5 * hk

    for layer in range(depth):
        final = layer == depth - 1
        mod = mod_all[layer]
        modsel = jnp.stack([jnp.broadcast_to(mod[bsz], (bsz, 6 * d)), mod[:bsz]],
                           axis=1).reshape(2 * bsz, 1, 6 * d)
        wl = w_in[layer]
        w_main = jnp.concatenate([wl[:, o_gt:], wl[:, o_qkv:o_b], wl[:, o_hq:o_gt]],
                                 axis=1).astype(BF16)
        wb = wl[:, o_b:o_a].reshape(d, 2, HEADS)
        wa = wl[:, o_a:o_hq].reshape(d, 2, HEADS)
        w_small_t = jnp.concatenate([wb, wa, jnp.zeros((d, 4, HEADS), F32)], axis=1)
        w_small_t = w_small_t.transpose(2, 1, 0).reshape(8 * HEADS, d)
        pad4 = jnp.zeros((HEADS, 4, 1), F32)
        zero2 = jnp.zeros((HEADS, 2, 1), F32)
        alog = jnp.concatenate([zero2, dn_a_log[layer].T[:, :, None], pad4], axis=1)
        dtb = jnp.concatenate([zero2, dn_dt_bias[layer].T[:, :, None], pad4], axis=1)

        if layer == 0:
            proj, small = _projection(xc, modsel, g_mix[layer][None, :], w_main, w_small_t,
                                      n_ctx_tiles, tm)
        else:
            xc, proj, small = _combine_projection(
                dest, y, route, x_mid, modsel_prev, modsel, g_mix[layer][None, :], w_main,
                w_small_t, n_ctx_tiles, tm)
        odn = _deltanet(proj, small, dn_conv[layer], alog, dtb, n_ctx)

        ohg = _hgrn2(proj, lower_bounds[layer][None, :], n_ctx)

        wr = jnp.zeros((d, 128), F32).at[:, :N_GROUPS].set(w_router_grp[layer])
        wr = wr.at[:, N_GROUPS:N_GROUPS + N_EXPERTS].set(w_router_exp[layer])
        br = jnp.zeros((1, 128), F32).at[0, :N_GROUPS].set(b_router_grp[layer])
        br = br.at[0, N_GROUPS:N_GROUPS + N_EXPERTS].set(b_router_exp[layer])
        x_mid, h2, route, counts = _merge(
            proj, odn, ohg, xc, modsel, w_br_dn[layer].astype(BF16), w_br_hg[layer].astype(BF16),
            w_out[layer].astype(BF16), dn_norm_g[layer][None, :], hg_norm_g[layer][None, :],
            g_ffn[layer][None, :], wr, br, n_ctx_tiles, tm)

        dest, block_expert, first, n_used, n_blocks = _routing_tables(
            route.reshape(n_tok, 128), counts, n_tok)
        rows = _dispatch(dest, h2.reshape(n_tok, d), n_blocks * MOE_BLOCK, tm)
        y = _experts(block_expert + layer * N_EXPERTS, first, n_used, rows,
                     w_exp_gate.reshape(depth * N_EXPERTS, d, -1),
                     w_exp_up.reshape(depth * N_EXPERTS, d, -1),
                     w_exp_down.reshape(depth * N_EXPERTS, -1, d))
        modsel_prev = modsel

    out = _combine(dest, y, route, x_mid, modsel, g_final[None, :], n_ctx_tiles, tm, True)
    return out[:, n_ctx:]
```

```python
import functools

import jax
import jax.numpy as jnp
from jax import lax
from jax.experimental import pallas as pl
from jax.experimental.pallas import tpu as pltpu

F32 = jnp.float32
BF16 = jnp.bfloat16
HI = lax.Precision.HIGHEST

GRID_W = 64
CHUNK = 64
EPS = 1e-6
F_TINY = 1e-30
HEADS = 4
DK = 128
N_GROUPS = 4
EXPERTS_PER_GROUP = 8
N_EXPERTS = 32
TOP_K = 2
MOE_BLOCK = 512
SUB = 16
EXP_CLAMP = 60.0
DN_INTRA_UNROLL = 3
HG_STEP_UNROLL = 4
HG_INTRA_UNROLL = 4
MERGE_COLS = 256
ROW_DMA_UNROLL = 8

C_GATES = 0
C_DNQKV = 2048
C_DNZ = 3584
C_HGQ = 4096
C_HGF = 4608
C_HGI = 5632
C_HGG = 6144
N_MAIN = 6656

VMEM_LIMIT = 56 * 1024 * 1024


def _cparams(sem):
    return pltpu.CompilerParams(dimension_semantics=sem, vmem_limit_bytes=VMEM_LIMIT)


def _silu(x):
    return x * jax.nn.sigmoid(x)


def _softplus(x):
    return jnp.maximum(x, 0.0) + jnp.log1p(jnp.exp(-jnp.abs(x)))


def _dot(a, b, prec=None):
    return jnp.dot(a, b, precision=prec, preferred_element_type=F32)


def _dot_nt(a, b, prec=None):
    return lax.dot_general(a, b, (((1,), (1,)), ((), ())), precision=prec,
                           preferred_element_type=F32)


def _dot_tn(a, b, prec=None):
    return lax.dot_general(a, b, (((0,), (0,)), ((), ())), precision=prec,
                           preferred_element_type=F32)


def _bdot(a, b):
    return _dot(a.astype(BF16), b.astype(BF16))


def _bdot_nt(a, b):
    return _dot_nt(a.astype(BF16), b.astype(BF16))


def _bdot_tn(a, b):
    return _dot_tn(a.astype(BF16), b.astype(BF16))


def _mod_kernel(cc_ref, w_ref, b_ref, o_ref):
    s = _silu(cc_ref[...])
    o_ref[0] = _dot(s, w_ref[0], HI) + b_ref[0]


def _modulation(cc, w_ada, b_ada):
    depth, d, n6 = w_ada.shape
    tn = n6 // 4
    return pl.pallas_call(
        _mod_kernel,
        grid=(depth, n6 // tn),
        in_specs=[pl.BlockSpec((16, d), lambda l, j: (0, 0)),
                  pl.BlockSpec((1, d, tn), lambda l, j: (l, 0, j)),
                  pl.BlockSpec((1, 1, tn), lambda l, j: (l, 0, j))],
        out_specs=pl.BlockSpec((1, 16, tn), lambda l, j: (l, 0, j)),
        out_shape=jax.ShapeDtypeStruct((depth, 16, n6), F32),
        compiler_params=_cparams(("parallel", "parallel")),
        name="modulation",
    )(cc, w_ada, b_ada.reshape(depth, 1, n6))


def _proj_kernel(x_ref, mod_ref, g_ref, w_ref, ws_ref, p_ref, s_ref):
    d = x_ref.shape[-1]
    x = x_ref[0]
    m = mod_ref[0]
    h = x * lax.rsqrt(jnp.mean(x * x, axis=-1, keepdims=True) + EPS) * g_ref[...]
    h = h * (1.0 + m[:, d:2 * d]) + m[:, :d]
    p_ref[0] = _dot(h.astype(BF16), w_ref[...])
    s_ref[0] = _dot_nt(ws_ref[...], h, HI)


def _combine_proj_kernel(dest_ref, y_ref, rt_ref, x_ref, modp_ref, mod_ref, g_ref, w_ref, ws_ref,
                         xo_ref, p_ref, s_ref, bufs, sems, *, tile):
    d = x_ref.shape[-1]
    steps_per_batch = pl.num_programs(1)
    step = pl.program_id(0) * steps_per_batch + pl.program_id(1)
    last = pl.num_programs(0) * steps_per_batch - 1
    slot = step % 2

    def row_copy(src, sl, k, r):
        return pltpu.make_async_copy(y_ref.at[pl.ds(src, 1)], bufs.at[sl, k, pl.ds(r, 1)],
                                     sems.at[sl])

    def issue_row(base, sl, r):
        t = base + r
        row_copy(dest_ref[TOP_K * t], sl, 0, r).start()
        row_copy(dest_ref[TOP_K * t + 1], sl, 1, r).start()

    def drain(sl):
        def wait_row(r, c):
            row_copy(0, sl, 0, 0).wait()
            row_copy(0, sl, 1, 0).wait()
            return c

        lax.fori_loop(0, tile, wait_row, 0, unroll=ROW_DMA_UNROLL)

    @pl.when(step == 0)
    def _():
        def first(r, c):
            issue_row(0, 0, r)
            return c

        lax.fori_loop(0, tile, first, 0, unroll=ROW_DMA_UNROLL)

    drain(slot)
    rt = rt_ref[0]
    f = rt[:, 2:3] * bufs[slot, 0] + rt[:, 3:4] * bufs[slot, 1]
    x = x_ref[0] + modp_ref[0][:, 5 * d:6 * d] * f
    xo_ref[0] = x

    next_base = jnp.minimum(step + 1, last) * tile
    for r in range(tile):
        issue_row(next_base, 1 - slot, r)

    m = mod_ref[0]
    h = x * lax.rsqrt(jnp.mean(x * x, axis=-1, keepdims=True) + EPS) * g_ref[...]
    h = h * (1.0 + m[:, d:2 * d]) + m[:, :d]
    p_ref[0] = _dot(h.astype(BF16), w_ref[...])
    s_ref[0] = _dot_nt(ws_ref[...], h, HI)

    @pl.when(step == last)
    def _():
        drain(1 - slot)


def _combine_projection(dest, y, route, x_mid, modsel_prev, modsel, g, w_main, w_small_t,
                        n_ctx_tiles, tm):
    bsz, n, d = x_mid.shape
    n_main = w_main.shape[1]
    tok = lambda w: pl.BlockSpec((1, tm, w), lambda b, i, dr: (b, i, 0))
    mod_spec = pl.BlockSpec(
        (1, 1, modsel.shape[-1]),
        lambda b, i, dr: (2 * b + (i >= n_ctx_tiles).astype(jnp.int32), 0, 0))
    return pl.pallas_call(
        functools.partial(_combine_proj_kernel, tile=tm),
        grid_spec=pltpu.PrefetchScalarGridSpec(
            num_scalar_prefetch=1,
            grid=(bsz, n // tm),
            in_specs=[pl.BlockSpec(memory_space=pl.ANY), tok(128), tok(d), mod_spec, mod_spec,
                      pl.BlockSpec((1, d), lambda b, i, dr: (0, 0)),
                      pl.BlockSpec((d, n_main), lambda b, i, dr: (0, 0),
                                   pipeline_mode=pl.Buffered(1)),
                      pl.BlockSpec((32, d), lambda b, i, dr: (0, 0))],
            out_specs=[tok(d), tok(n_main),
                       pl.BlockSpec((1, 32, tm), lambda b, i, dr: (b, 0, i))],
            scratch_shapes=[pltpu.VMEM((2, TOP_K, tm, d), F32), pltpu.SemaphoreType.DMA((2,))]),
        out_shape=[jax.ShapeDtypeStruct((bsz, n, d), F32),
                   jax.ShapeDtypeStruct((bsz, n, n_main), F32),
                   jax.ShapeDtypeStruct((bsz, 32, n), F32)],
        compiler_params=_cparams(("arbitrary", "arbitrary")),
        name="combine_projection",
    )(dest, y, route, x_mid, modsel_prev, modsel, g, w_main, w_small_t)


def _projection(xc, modsel, g, w_main, w_small_t, n_ctx_tiles, tm):
    bsz, n, d = xc.shape
    n_main = w_main.shape[1]
    return pl.pallas_call(
        _proj_kernel,
        grid=(bsz, n // tm),
        in_specs=[pl.BlockSpec((1, tm, d), lambda b, i: (b, i, 0)),
                  pl.BlockSpec((1, 1, modsel.shape[-1]),
                               lambda b, i: (2 * b + (i >= n_ctx_tiles).astype(jnp.int32), 0, 0)),
                  pl.BlockSpec((1, d), lambda b, i: (0, 0)),
                  pl.BlockSpec((d, n_main), lambda b, i: (0, 0), pipeline_mode=pl.Buffered(1)),
                  pl.BlockSpec((32, d), lambda b, i: (0, 0))],
        out_specs=[pl.BlockSpec((1, tm, n_main), lambda b, i: (b, i, 0)),
                   pl.BlockSpec((1, 32, tm), lambda b, i: (b, 0, i))],
        out_shape=[jax.ShapeDtypeStruct((bsz, n, n_main), F32),
                   jax.ShapeDtypeStruct((bsz, 32, n), F32)],
        compiler_params=_cparams(("parallel", "parallel")),
        name="projection",
    )(xc, modsel, g, w_main, w_small_t)


def _tri_masks(fwd):
    ii = lax.broadcasted_iota(jnp.int32, (CHUNK, CHUNK), 0)
    jj = lax.broadcasted_iota(jnp.int32, (CHUNK, CHUNK), 1)
    if fwd:
        return ii >= jj, ii > jj
    return ii <= jj, ii < jj


def _unit_tri_solve(a, rhs, mm):
    ii = lax.broadcasted_iota(jnp.int32, a[0].shape, 0)
    jj = lax.broadcasted_iota(jnp.int32, a[0].shape, 1)
    same = (ii // SUB) == (jj // SUB)
    eye = jnp.where(ii == jj, 1.0, 0.0)
    each = lambda f, *xs: [f(*t) for t in zip(*xs)]
    axpy = lambda x, y: each(lambda xi, yi: xi + mm(xi, yi), x, y)
    dm = each(lambda t: jnp.where(same, t, 0.0), a)
    lm = each(lambda t: jnp.where(same, 0.0, t), a)
    d2 = each(mm, dm, dm)
    d4 = each(mm, d2, d2)
    d8 = each(mm, d4, d4)
    p = axpy(each(lambda t: eye - t, dm), d2)
    p = axpy(p, d4)
    td = axpy(p, d8)
    m = each(mm, td, lm)
    m2 = each(mm, m, m)
    t1 = each(mm, td, rhs)
    t2 = each(lambda x, y: x + mm(y, x), t1, m2)
    return each(lambda x, y: x - mm(y, x), t2, m)


def _chunk_schedule(s, n_ctx_chunks, n_chunks):
    cf = s
    cb = jnp.where(s < n_ctx_chunks, n_ctx_chunks - 1 - s, n_chunks + n_ctx_chunks - 1 - s)
    return pl.multiple_of(cf * CHUNK, CHUNK), pl.multiple_of(cb * CHUNK, CHUNK)


def _seq_cumsum(x, fwd):
    n = x.shape[0]
    pos = lax.broadcasted_iota(jnp.int32, x.shape, 0) % CHUNK
    s = 1
    while s < CHUNK:
        if fwd:
            x = x + jnp.where(pos >= s, pltpu.roll(x, s, 0), 0.0)
        else:
            x = x + jnp.where(pos < CHUNK - s, pltpu.roll(x, n - s, 0), 0.0)
        s *= 2
    return x


def _dn_intra(q, k, v, beta, gc, fwds):
    m = q[0].shape[0]
    ii = lax.broadcasted_iota(jnp.int32, (m, m), 0)
    jj = lax.broadcasted_iota(jnp.int32, (m, m), 1)
    blk = ii // CHUNK
    same = blk == (jj // CHUNK)
    is_fwd = functools.reduce(lambda x, y: x | y,
                              [blk == g for g, f in enumerate(fwds) if f], blk < 0)
    incl = same & ((is_fwd & (ii >= jj)) | (~is_fwd & (ii <= jj)))
    strict = incl & (ii != jj)
    wide = lambda t: jnp.concatenate([t] * (m // DK), axis=1)
    tall = lambda t: jnp.concatenate([t] * (m // DK), axis=0)
    each = lambda f, *xs: [f(*t) for t in zip(*xs)]
    kk = each(_bdot_nt, k, k)
    qk = each(_bdot_nt, q, k)
    dec = each(lambda g: jnp.where(incl, jnp.exp(jnp.where(incl, wide(g) - tall(g.T), 0.0)), 0.0), gc)
    a = each(lambda x, b, d: jnp.where(strict, x * wide(b) * d, 0.0), kk, beta, dec)
    eg = each(jnp.exp, gc)
    rhs = each(lambda vi, ki, b, e: jnp.concatenate([vi * b, ki * (b * e)], axis=1), v, k, beta, eg)
    sol = _unit_tri_solve(a, rhs, _bdot)

    def to_end(g):
        rows = [g[i * CHUNK + (CHUNK - 1 if f else 0)][None, :] for i, f in enumerate(fwds)]
        return jnp.concatenate([jnp.broadcast_to(r, (CHUNK, DK)) for r in rows], axis=0) - g

    n_stack = len(fwds)
    ri = lax.broadcasted_iota(jnp.int32, (n_stack * DK, m), 0) // DK
    ci = lax.broadcasted_iota(jnp.int32, (n_stack * DK, m), 1) // CHUNK
    kdt_bd = each(lambda ki, g: jnp.where(
        ri == ci, jnp.concatenate([(ki * jnp.exp(to_end(g))).T] * n_stack, axis=0), 0.0), k, gc)
    att = each(lambda x, d: jnp.where(incl, x * d, 0.0), qk, dec)
    att_uw = each(_bdot, att, sol)
    kd_uw = each(_bdot, kdt_bd, sol)
    q_eff = each(lambda x, e, aw: x * e - aw[:, DK:], q, eg, att_uw)
    decay = each(lambda g: [jnp.exp(g[i * CHUNK + (CHUNK - 1 if f else 0)][None, :])
                            for i, f in enumerate(fwds)], gc)
    return [(aw[:, :DK], qe, kw, dc) for aw, qe, kw, dc in zip(att_uw, q_eff, kd_uw, decay)]


DN_PAIR_ORDER = ((0, 2), (3, 1))


def _dn_compose(stacks):
    blk = lambda t, g, size: t[g * size:(g + 1) * size]
    lhs, rhs = [], []
    for o0, qe, kw, dc in stacks:
        for first, second in DN_PAIR_ORDER:
            lhs.append(jnp.concatenate([blk(kw, second, DK)[:, DK:], blk(qe, second, CHUNK)], axis=0))
            rhs.append(blk(kw, first, DK))
    z = [_bdot(a, b) for a, b in zip(lhs, rhs)]
    out = []
    for s, (o0, qe, kw, dc) in enumerate(stacks):
        o0b = [blk(o0, g, CHUNK) for g in range(4)]
        qb = [blk(qe, g, CHUNK) for g in range(4)]
        pairs = []
        for di, (first, second) in enumerate(DN_PAIR_ORDER):
            zi = z[2 * s + di]
            k1, k2 = blk(kw, first, DK), blk(kw, second, DK)
            e1, e2 = dc[first], dc[second]
            c12 = e2 * k1[:, :DK] + k2[:, :DK] - zi[:DK, :DK]
            n12 = e2 * k1[:, DK:] + e1 * k2[:, DK:] - zi[:DK, DK:]
            o0b[second] = o0b[second] + zi[DK:, :DK]
            qb[second] = e1 * qb[second] - zi[DK:, DK:]
            pairs.append((c12, n12, e1 * e2))
        out.append((o0b, qb, pairs))
    return out


def _dn_kernel(q_ref, k_ref, v_ref, cq_ref, ck_ref, cv_ref, sm_ref, alog_ref, dtb_ref,
               o_ref, qs, ks, vs, bfs, bbs, gfs, gbs, qes, cs, ns, es, *, n_ctx):
    n = q_ref.shape[1]
    row = lax.broadcasted_iota(jnp.int32, (n, DK), 0)
    seg_first = (row == 0) | (row == n_ctx)
    seg_last = (row == n_ctx - 1) | (row == n - 1)

    def conv_act(x_ref, c_ref):
        x = x_ref[0]
        w = c_ref[...]
        xm = jnp.where(seg_first, 0.0, pltpu.roll(x, 1, 0))
        xp = jnp.where(seg_last, 0.0, pltpu.roll(x, n - 1, 0))
        return _silu(xm * w[0:1] + x * w[1:2] + xp * w[2:3])

    def l2n(y):
        return y * lax.rsqrt(jnp.sum(y * y, axis=-1, keepdims=True) + EPS)

    qs[...] = l2n(conv_act(q_ref, cq_ref)) * (DK ** -0.5)
    ks[...] = l2n(conv_act(k_ref, ck_ref))
    vs[...] = conv_act(v_ref, cv_ref)

    sm = sm_ref[0]
    beta = jax.nn.sigmoid(sm)
    g = -jnp.exp(alog_ref[0]) * _softplus(sm + dtb_ref[0])
    pos = lax.broadcasted_iota(jnp.int32, sm.shape, 1) % CHUNK
    pre, suf = g, g
    s = 1
    while s < CHUNK:
        pre = pre + jnp.where(pos >= s, pltpu.roll(pre, s, 1), 0.0)
        suf = suf + jnp.where(pos < CHUNK - s, pltpu.roll(suf, n - s, 1), 0.0)
        s *= 2

    def col(r):
        return jnp.broadcast_to(r, (DK, n)).T

    bfs[...] = col(beta[0:1])
    bbs[...] = col(beta[1:2])
    gfs[...] = col(pre[2:3])
    gbs[...] = col(suf[3:4])

    n_chunks = n // CHUNK
    n_ctx_chunks = n_ctx // CHUNK

    def intra(c, carry):
        starts = [pl.multiple_of((c * DN_INTRA_UNROLL + j) * 2 * CHUNK, 2 * CHUNK)
                  for j in range(DN_INTRA_UNROLL)]
        pairs = [pl.ds(r0, 2 * CHUNK) for r0 in starts]
        stack = lambda t: jnp.concatenate([t[:CHUNK], t[:CHUNK], t[CHUNK:], t[CHUNK:]], axis=0)
        both = lambda tf, tb: jnp.concatenate(
            [tf[:CHUNK], tb[:CHUNK], tf[CHUNK:], tb[CHUNK:]], axis=0)
        results = _dn_intra(
            [stack(qs[p, :]) for p in pairs], [stack(ks[p, :]) for p in pairs],
            [stack(vs[p, :]) for p in pairs],
            [both(bfs[p, :], bbs[p, :]) for p in pairs],
            [both(gfs[p, :], gbs[p, :]) for p in pairs], (True, False, True, False))
        for r0, (o0, q_eff, pair_maps) in zip(starts, _dn_compose(results)):
            pair = pl.ds(r0, 2 * CHUNK)
            o_ref[0, pair, :] = jnp.concatenate([o0[0] + o0[1], o0[2] + o0[3]], axis=0)
            for di, (c12, n12, e12) in enumerate(pair_maps):
                qes[di, pair, :] = jnp.concatenate([q_eff[di], q_eff[2 + di]], axis=0).astype(BF16)
                cs[di, pair, :] = c12
                ns[di, pair, :] = n12.astype(BF16)
                es[di, pl.ds(r0 // (2 * CHUNK), 1), :] = e12
        return carry

    lax.fori_loop(0, n_chunks // (2 * DN_INTRA_UNROLL), intra, 0)

    def step(s, states):
        first_b = jnp.where(2 * s < n_ctx_chunks, n_ctx_chunks - 1 - 2 * s,
                            n_chunks + n_ctx_chunks - 1 - 2 * s)
        pair_idx = (s, first_b // 2)
        pairs = [pl.ds(pl.multiple_of(p * 2 * CHUNK, 2 * CHUNK), 2 * CHUNK) for p in pair_idx]
        stb = [st.astype(BF16) for st in states]
        corr = [_dot(ns[di, pairs[di], :], stb[di]) for di in range(2)]
        outs = [_dot(qes[di, pairs[di], :], stb[di]) for di in range(2)]
        new_states = []
        for di in range(2):
            decay = es[di, pl.ds(pair_idx[di], 1), :]
            new_states.append(states[di] * decay + cs[di, pairs[di], :] - corr[di])
            o_ref[0, pairs[di], :] = o_ref[0, pairs[di], :] + outs[di]
        return tuple(new_states)

    zero = jnp.zeros((DK, DK), F32)
    lax.fori_loop(0, n_chunks // 2, step, (zero, zero))


def _deltanet(proj, small, conv_w, alog, dtb, n_ctx):
    bsz, n, _ = proj.shape
    qb = C_DNQKV // DK
    seq = lambda off: pl.BlockSpec((1, n, DK), lambda b, h: (b, 0, off + h))
    cw = lambda off: pl.BlockSpec((3, DK), lambda b, h: (0, off + h))
    return pl.pallas_call(
        functools.partial(_dn_kernel, n_ctx=n_ctx),
        grid=(bsz, HEADS),
        in_specs=[seq(qb), seq(qb + HEADS), seq(qb + 2 * HEADS),
                  cw(0), cw(HEADS), cw(2 * HEADS),
                  pl.BlockSpec((1, 8, n), lambda b, h: (b, h, 0)),
                  pl.BlockSpec((1, 8, 1), lambda b, h: (h, 0, 0)),
                  pl.BlockSpec((1, 8, 1), lambda b, h: (h, 0, 0))],
        out_specs=pl.BlockSpec((1, n, DK), lambda b, h: (b, 0, h)),
        out_shape=jax.ShapeDtypeStruct((bsz, n, HEADS * DK), F32),
        scratch_shapes=[pltpu.VMEM((n, DK), F32) for _ in range(7)] + [
            pltpu.VMEM((2, n, DK), BF16),
            pltpu.VMEM((2, n, DK), F32),
            pltpu.VMEM((2, n, DK), BF16),
            pltpu.VMEM((2, -(-n // (16 * CHUNK)) * 8, DK), F32)],
        compiler_params=_cparams(("parallel", "parallel")),
        name="deltanet",
    )(proj, proj, proj, conv_w, conv_w, conv_w, small, alog, dtb)


HG_FWDS = (True, False, True, False)


def _hg_intra(q, zf, zb, v, lb, causal_ref, blockdiag_ref):
    fwds = HG_FWDS
    n_stack = len(fwds)
    nsub = CHUNK // SUB

    def gate(z):
        f = lb + (1.0 - lb) * jax.nn.sigmoid(z)
        return (1.0 - lb) * jax.nn.sigmoid(-z), jnp.log(jnp.maximum(f, F_TINY))

    kf_f, lf_f = gate(zf)
    kf_b, lf_b = gate(zb)
    both = lambda tf, tb: jnp.concatenate([tf[:CHUNK], tb[:CHUNK], tf[CHUNK:], tb[CHUNK:]], axis=0)
    stack = lambda t: both(t, t)
    q_st = stack(_silu(q))
    v_st = stack(v)
    kf = both(kf_f, kf_b)
    gc = both(_seq_cumsum(lf_f, True), _seq_cumsum(lf_b, False))

    def ref_row(g, i):
        r = g * CHUNK + i * SUB + (0 if fwds[g] else SUB - 1)
        return gc[r:r + 1, :]

    rep = lambda r, k: jnp.broadcast_to(r, (k, DK))
    qe = q_st * jnp.exp(gc - jnp.concatenate(
        [rep(ref_row(g, i), SUB) for g in range(n_stack) for i in range(nsub)], axis=0))
    scores = []
    for i in range(nsub):
        ref = jnp.concatenate([rep(ref_row(g, i), CHUNK) for g in range(n_stack)], axis=0)
        ke = kf * jnp.exp(jnp.minimum(ref - gc, EXP_CLAMP))
        lhs = jnp.concatenate([qe[g * CHUNK + i * SUB:g * CHUNK + (i + 1) * SUB]
                               for g in range(n_stack)], axis=0)
        scores.append(_bdot_nt(lhs, ke))
    att = jnp.concatenate([scores[i][g * SUB:(g + 1) * SUB]
                           for g in range(n_stack) for i in range(nsub)], axis=0)
    o0 = _bdot(att * causal_ref[...], v_st)

    end_rows = [gc[g * CHUNK + (CHUNK - 1 if f else 0)][None, :] for g, f in enumerate(fwds)]
    kd = kf * jnp.exp(jnp.concatenate([rep(r, CHUNK) for r in end_rows], axis=0) - gc)
    vt_bd = jnp.concatenate([v_st.T] * n_stack, axis=0) * blockdiag_ref[...]
    ct = _bdot(vt_bd, kd)
    return o0, q_st * jnp.exp(gc), ct, [jnp.exp(r) for r in end_rows]


def _hg_masks(causal_ref, blockdiag_ref):
    m = len(HG_FWDS) * CHUNK
    ii = lax.broadcasted_iota(jnp.int32, (m, m), 0)
    jj = lax.broadcasted_iota(jnp.int32, (m, m), 1)
    blk = ii // CHUNK
    is_fwd = functools.reduce(lambda x, y: x | y,
                              [blk == g for g, f in enumerate(HG_FWDS) if f], blk < 0)
    incl = (blk == (jj // CHUNK)) & ((is_fwd & (ii >= jj)) | (~is_fwd & (ii <= jj)))
    causal_ref[...] = jnp.where(incl, 1.0, 0.0)
    ri = lax.broadcasted_iota(jnp.int32, blockdiag_ref.shape, 0) // DK
    ci = lax.broadcasted_iota(jnp.int32, blockdiag_ref.shape, 1) // CHUNK
    blockdiag_ref[...] = jnp.where(ri == ci, 1.0, 0.0)


def _hg_kernel(q_ref, zf_ref, zb_ref, v_ref, lb_ref, o_ref, oscan, qgs, cts, es, causal, blockdiag,
               *, n_ctx):
    n = q_ref.shape[1]
    n_lat = n - n_ctx
    col_len = n_lat // GRID_W
    cols_per_pair = 2 * CHUNK // col_len
    lb = lb_ref[...]
    _hg_masks(causal, blockdiag)

    def store(r0, res):
        o0, qg, ct, decay = res
        for g in range(4):
            di = g % 2
            rg = r0 + (g // 2) * CHUNK
            sl = pl.ds(rg, CHUNK)
            if di == 0:
                oscan[sl, :] = o0[g * CHUNK:(g + 1) * CHUNK] + o0[(g + 1) * CHUNK:(g + 2) * CHUNK]
            qgs[di, sl, :] = qg[g * CHUNK:(g + 1) * CHUNK].astype(BF16)
            cts[di, pl.ds(pl.multiple_of(2 * rg, 2 * CHUNK), DK), :] = ct[g * DK:(g + 1) * DK]
            es[di, pl.ds(rg // CHUNK, 1), :] = decay[g]

    def ctx_pair(i, c):
        r0 = pl.multiple_of(i * 2 * CHUNK, 2 * CHUNK)
        sl = pl.ds(r0, 2 * CHUNK)
        store(r0, _hg_intra(q_ref[0, sl, :], zf_ref[0, sl, :], zb_ref[0, sl, :], v_ref[0, sl, :], lb,
                            causal, blockdiag))
        return c

    lax.fori_loop(0, n_ctx // (2 * CHUNK), ctx_pair, 0, unroll=HG_INTRA_UNROLL)

    def lat_pair(i, c):
        def load(ref):
            return jnp.concatenate(
                [ref[0, pl.ds(n_ctx + cols_per_pair * i + j, col_len, stride=GRID_W), :]
                 for j in range(cols_per_pair)], axis=0)

        r0 = pl.multiple_of(n_ctx + i * 2 * CHUNK, 2 * CHUNK)
        store(r0, _hg_intra(load(q_ref), load(zf_ref), load(zb_ref), load(v_ref), lb,
                            causal, blockdiag))
        return c

    lax.fori_loop(0, n_lat // (2 * CHUNK), lat_pair, 0, unroll=HG_INTRA_UNROLL)

    n_chunks = n // CHUNK
    n_ctx_chunks = n_ctx // CHUNK

    def steps(t, states):
        for j in range(HG_STEP_UNROLL):
            rows = _chunk_schedule(t * HG_STEP_UNROLL + j, n_ctx_chunks, n_chunks)
            sls = [pl.ds(r0, CHUNK) for r0 in rows]
            outs = [_dot_nt(qgs[di, sls[di], :], states[di].astype(BF16)) for di in range(2)]
            new_states = []
            for di in range(2):
                decay = es[di, pl.ds(rows[di] // CHUNK, 1), :]
                c = cts[di, pl.ds(pl.multiple_of(2 * rows[di], 2 * CHUNK), DK), :]
                new_states.append(states[di] * decay + c)
                oscan[sls[di], :] = oscan[sls[di], :] + outs[di]
            states = tuple(new_states)
        return states

    zero = jnp.zeros((DK, DK), F32)
    lax.fori_loop(0, n_chunks // HG_STEP_UNROLL, steps, (zero, zero))

    o_ref[0, pl.ds(0, n_ctx), :] = oscan[pl.ds(0, n_ctx), :]

    def grid_row(r, c):
        dst = pl.ds(pl.multiple_of(n_ctx + r * GRID_W, GRID_W), GRID_W)
        o_ref[0, dst, :] = oscan[pl.ds(n_ctx + r, GRID_W, stride=col_len), :]
        return c

    lax.fori_loop(0, col_len, grid_row, 0)


def _hgrn2(proj, lb, n_ctx):
    bsz, n, _ = proj.shape
    seq = lambda off: pl.BlockSpec((1, n, DK), lambda b, h: (b, 0, off // DK + h))
    n_chunks = n // CHUNK
    return pl.pallas_call(
        functools.partial(_hg_kernel, n_ctx=n_ctx),
        grid=(bsz, HEADS),
        in_specs=[seq(C_HGQ), seq(C_HGF), seq(C_HGF + HEADS * DK), seq(C_HGI),
                  pl.BlockSpec((1, DK), lambda b, h: (0, h))],
        out_specs=pl.BlockSpec((1, n, DK), lambda b, h: (b, 0, h)),
        out_shape=jax.ShapeDtypeStruct((bsz, n, HEADS * DK), F32),
        scratch_shapes=[pltpu.VMEM((n, DK), F32),
                        pltpu.VMEM((2, n, DK), BF16),
                        pltpu.VMEM((2, 2 * n, DK), F32),
                        pltpu.VMEM((2, -(-n_chunks // 8) * 8, DK), F32),
                        pltpu.VMEM((len(HG_FWDS) * CHUNK, len(HG_FWDS) * CHUNK), F32),
                        pltpu.VMEM((len(HG_FWDS) * DK, len(HG_FWDS) * CHUNK), F32)],
        compiler_params=_cparams(("parallel", "parallel")),
        name="hgrn2",
    )(proj, proj, proj, proj, lb)


def _gated_rmsnorm(o, z, g):
    parts = []
    for h in range(HEADS):
        oh = o[:, h * DK:(h + 1) * DK]
        zh = z[:, h * DK:(h + 1) * DK]
        y = oh * lax.rsqrt(jnp.mean(oh * oh, axis=-1, keepdims=True) + EPS) * g
        parts.append(y * _silu(zh))
    return jnp.concatenate(parts, axis=1)


def _route(lg):
    lane = lax.broadcasted_iota(jnp.int32, lg.shape, 1).astype(F32)
    neg = -1e30
    big = 1e9
    is_grp = lane < N_GROUPS
    gl = jnp.where(is_grp, lg, neg)
    gmax = jnp.max(gl, axis=-1, keepdims=True)
    gsel = jnp.min(jnp.where(gl == gmax, lane, big), axis=-1, keepdims=True)
    gp = 1.0 / jnp.sum(jnp.where(is_grp, jnp.exp(gl - gmax), 0.0), axis=-1, keepdims=True)
    lo = N_GROUPS + EXPERTS_PER_GROUP * gsel
    el = jnp.where((lane >= lo) & (lane < lo + EXPERTS_PER_GROUP), lg, neg)
    m1 = jnp.max(el, axis=-1, keepdims=True)
    i1 = jnp.min(jnp.where(el == m1, lane, big), axis=-1, keepdims=True)
    el2 = jnp.where(lane == i1, neg, el)
    m2 = jnp.max(el2, axis=-1, keepdims=True)
    i2 = jnp.min(jnp.where(el2 == m2, lane, big), axis=-1, keepdims=True)
    r = jnp.exp(m2 - m1)
    g1 = gp / (1.0 + r)
    g2 = g1 * r
    out = jnp.where(lane == 0, i1 - N_GROUPS, 0.0)
    out = jnp.where(lane == 1, i2 - N_GROUPS, out)
    out = jnp.where(lane == 2, g1, out)
    return jnp.where(lane == 3, g2, out)


def _merge_kernel(gate_ref, z_ref, hg_ref, odn_ref, ohg_ref, x_ref, mod_ref, wdn_ref, whg_ref,
                  wout_ref, gdn_ref, ghg_ref, gffn_ref, wrh_ref, wrl_ref, br_ref,
                  xo_ref, h2_ref, rt_ref, cnt_ref, mix_s):
    d = x_ref.shape[-1]
    cols = [slice(j * MERGE_COLS, (j + 1) * MERGE_COLS) for j in range(d // MERGE_COLS)]
    mod = lambda k, cs: mod_ref[0, :, k * d + cs.start:k * d + cs.stop]

    @pl.when((pl.program_id(0) == 0) & (pl.program_id(1) == 0))
    def _():
        cnt_ref[...] = jnp.zeros(cnt_ref.shape, F32)

    a_dn = _gated_rmsnorm(odn_ref[0], z_ref[0], gdn_ref[...]).astype(BF16)
    a_hg = _gated_rmsnorm(ohg_ref[0], hg_ref[0], ghg_ref[...]).astype(BF16)
    for cs in cols:
        br_dn = _dot(a_dn, wdn_ref[:, cs])
        br_hg = _dot(a_hg, whg_ref[:, cs])
        g_dn = jax.nn.sigmoid(gate_ref[0, :, cs])
        g_hg = jax.nn.sigmoid(gate_ref[0, :, d + cs.start:d + cs.stop])
        mix_s[:, cs] = (g_dn * br_dn + g_hg * br_hg).astype(BF16)
    mix = mix_s[...]
    ssq = jnp.zeros((x_ref.shape[1], 1), F32)
    for cs in cols:
        x = x_ref[0, :, cs] + mod(2, cs) * _dot(mix, wout_ref[:, cs])
        xo_ref[0, :, cs] = x
        ssq = ssq + jnp.sum(x * x, axis=-1, keepdims=True)
    scale = lax.rsqrt(ssq * (1.0 / d) + EPS)
    lg = jnp.zeros(rt_ref.shape[1:], F32) + br_ref[...]
    for cs in cols:
        h2 = xo_ref[0, :, cs] * scale * gffn_ref[:, cs] * (1.0 + mod(4, cs)) + mod(3, cs)
        h2_ref[0, :, cs] = h2
        h_hi = h2.astype(BF16)
        h_lo = (h2 - h_hi.astype(F32)).astype(BF16)
        lg = lg + (_dot(h_hi, wrh_ref[cs, :]) + (_dot(h_hi, wrl_ref[cs, :]) + _dot(h_lo, wrh_ref[cs, :])))
    rt = _route(lg)
    tm = rt.shape[0]
    lane = lax.broadcasted_iota(jnp.int32, rt.shape, 1)
    lanef = lane.astype(F32)
    oh1 = jnp.where(lanef == rt[:, 0:1], 1.0, 0.0)
    oh2 = jnp.where(lanef == rt[:, 1:2], 1.0, 0.0)
    both = oh1 + oh2
    ii = lax.broadcasted_iota(jnp.int32, (tm, tm), 0)
    jj = lax.broadcasted_iota(jnp.int32, (tm, tm), 1)
    earlier = jnp.where(ii > jj, 1.0, 0.0).astype(BF16)
    prior = _dot(earlier, both.astype(BF16)) + cnt_ref[0:1, :]
    rank1 = jnp.sum(prior * oh1, axis=-1, keepdims=True)
    rank2 = jnp.sum(prior * oh2, axis=-1, keepdims=True)
    rt = jnp.where(lane == 4, rank1, rt)
    rt_ref[0] = jnp.where(lane == 5, rank2, rt)
    cnt_ref[0:1, :] = cnt_ref[0:1, :] + jnp.sum(both, axis=0, keepdims=True)


def _merge(proj, odn, ohg, xc, modsel, wdn, whg, wout, gdn, ghg, gffn, wr, br, n_ctx_tiles, tm):
    wr_hi = wr.astype(BF16)
    wr_lo = (wr - wr_hi.astype(F32)).astype(BF16)
    bsz, n, d = xc.shape
    hv = HEADS * DK
    tok = lambda w, j: pl.BlockSpec((1, tm, w), lambda b, i: (b, i, j))
    full = lambda a: pl.BlockSpec(a.shape, lambda b, i: (0,) * a.ndim)
    return pl.pallas_call(
        _merge_kernel,
        grid=(bsz, n // tm),
        in_specs=[tok(2 * d, C_GATES // (2 * d)), tok(hv, C_DNZ // hv), tok(hv, C_HGG // hv),
                  tok(hv, 0), tok(hv, 0), tok(d, 0),
                  pl.BlockSpec((1, 1, modsel.shape[-1]),
                               lambda b, i: (2 * b + (i >= n_ctx_tiles).astype(jnp.int32), 0, 0)),
                  full(wdn), full(whg), full(wout), full(gdn), full(ghg), full(gffn),
                  full(wr_hi), full(wr_lo), full(br)],
        out_specs=[tok(d, 0), tok(d, 0), tok(128, 0),
                   pl.BlockSpec((8, 128), lambda b, i: (0, 0))],
        out_shape=[jax.ShapeDtypeStruct((bsz, n, d), F32),
                   jax.ShapeDtypeStruct((bsz, n, d), F32),
                   jax.ShapeDtypeStruct((bsz, n, 128), F32),
                   jax.ShapeDtypeStruct((8, 128), F32)],
        scratch_shapes=[pltpu.VMEM((tm, d), BF16)],
        compiler_params=_cparams(("arbitrary", "arbitrary")),
        name="merge",
    )(proj, proj, proj, odn, ohg, xc, modsel, wdn, whg, wout, gdn, ghg, gffn, wr_hi, wr_lo, br)


def _dispatch_kernel(dest_ref, h_ref, init_ref, rows_ref, sem, *, tile):
    del init_ref
    base = pl.program_id(0) * tile

    def row_copy(r, d):
        return pltpu.make_async_copy(h_ref.at[pl.ds(r, 1)], rows_ref.at[pl.ds(d, 1)], sem)

    def issue(r, c):
        t = base + r
        row_copy(r, dest_ref[TOP_K * t]).start()
        row_copy(r, dest_ref[TOP_K * t + 1]).start()
        return c

    lax.fori_loop(0, tile, issue, 0, unroll=ROW_DMA_UNROLL)

    def drain(r, c):
        row_copy(0, 0).wait()
        row_copy(0, 0).wait()
        return c

    lax.fori_loop(0, tile, drain, 0, unroll=ROW_DMA_UNROLL)


def _dispatch(dest, h2, n_rows, tile):
    n_tok, d = h2.shape
    return pl.pallas_call(
        functools.partial(_dispatch_kernel, tile=tile),
        grid_spec=pltpu.PrefetchScalarGridSpec(
            num_scalar_prefetch=1,
            grid=(n_tok // tile,),
            in_specs=[pl.BlockSpec((tile, d), lambda i, dr: (i, 0)),
                      pl.BlockSpec(memory_space=pl.ANY)],
            out_specs=pl.BlockSpec(memory_space=pl.ANY),
            scratch_shapes=[pltpu.SemaphoreType.DMA(())]),
        out_shape=jax.ShapeDtypeStruct((n_rows, d), F32),
        input_output_aliases={2: 0},
        compiler_params=pltpu.CompilerParams(dimension_semantics=("arbitrary",),
                                             has_side_effects=True),
        name="moe_dispatch",
    )(dest, h2, jnp.zeros((n_rows, d), F32))


def _expert_kernel(be_ref, first_ref, nused_ref, x_ref, wg_ref, wu_ref, wd_ref, o_ref,
                   wg_s, wu_s, wd_s):
    i = pl.program_id(0)

    @pl.when(first_ref[i] == 1)
    def _():
        wg_s[...] = wg_ref[0].astype(BF16)
        wu_s[...] = wu_ref[0].astype(BF16)
        wd_s[...] = wd_ref[0].astype(BF16)

    @pl.when(i < nused_ref[0])
    def _():
        x = x_ref[...].astype(BF16)
        a = _silu(_dot(x, wg_s[...])) * _dot(x, wu_s[...])
        o_ref[...] = _dot(a.astype(BF16), wd_s[...])

    @pl.when(i >= nused_ref[0])
    def _():
        o_ref[...] = jnp.zeros(o_ref.shape, F32)


def _experts(block_expert, first, n_used, rows, wg, wu, wd):
    n_rows, d = rows.shape
    de = wg.shape[-1]
    return pl.pallas_call(
        _expert_kernel,
        grid_spec=pltpu.PrefetchScalarGridSpec(
            num_scalar_prefetch=3,
            grid=(n_rows // MOE_BLOCK,),
            in_specs=[pl.BlockSpec((MOE_BLOCK, d), lambda i, be, f, nu: (jnp.minimum(i, nu[0] - 1), 0)),
                      pl.BlockSpec((1, d, de), lambda i, be, f, nu: (be[i], 0, 0)),
                      pl.BlockSpec((1, d, de), lambda i, be, f, nu: (be[i], 0, 0)),
                      pl.BlockSpec((1, de, d), lambda i, be, f, nu: (be[i], 0, 0))],
            out_specs=pl.BlockSpec((MOE_BLOCK, d), lambda i, be, f, nu: (i, 0)),
            scratch_shapes=[pltpu.VMEM((d, de), BF16), pltpu.VMEM((d, de), BF16),
                            pltpu.VMEM((de, d), BF16)]),
        out_shape=jax.ShapeDtypeStruct((n_rows, d), F32),
        compiler_params=_cparams(("arbitrary",)),
        name="moe_experts",
    )(block_expert, first, n_used, rows, wg, wu, wd)


def _combine_kernel(dest_ref, y_ref, rt_ref, x_ref, mod_ref, gfin_ref, o_ref, buf0, buf1, sem,
                    *, tile, tiles_per_batch, first_tile):
    d = x_ref.shape[-1]
    base = (pl.program_id(0) * tiles_per_batch + first_tile + pl.program_id(1)) * tile

    def row_copy(src, buf, r):
        return pltpu.make_async_copy(y_ref.at[pl.ds(src, 1)], buf.at[pl.ds(r, 1)], sem)

    def issue(r, c):
        t = base + r
        row_copy(dest_ref[TOP_K * t], buf0, r).start()
        row_copy(dest_ref[TOP_K * t + 1], buf1, r).start()
        return c

    lax.fori_loop(0, tile, issue, 0, unroll=ROW_DMA_UNROLL)

    def drain(r, c):
        row_copy(0, buf0, 0).wait()
        row_copy(0, buf1, 0).wait()
        return c

    lax.fori_loop(0, tile, drain, 0, unroll=ROW_DMA_UNROLL)

    rt = rt_ref[0]
    f = rt[:, 2:3] * buf0[...] + rt[:, 3:4] * buf1[...]
    x = x_ref[0] + mod_ref[0][:, 5 * d:6 * d] * f
    o_ref[0] = x * lax.rsqrt(jnp.mean(x * x, axis=-1, keepdims=True) + EPS) * gfin_ref[...]


def _final_combine(dest, y, route, xc, modsel, g_final, n_ctx_tiles, tm):
    bsz, n, d = xc.shape
    n_tiles = n // tm
    tok = lambda w: pl.BlockSpec((1, tm, w), lambda b, i, dr: (b, i + n_ctx_tiles, 0))
    return pl.pallas_call(
        functools.partial(_combine_kernel, tile=tm, tiles_per_batch=n_tiles,
                          first_tile=n_ctx_tiles),
        grid_spec=pltpu.PrefetchScalarGridSpec(
            num_scalar_prefetch=1,
            grid=(bsz, n_tiles - n_ctx_tiles),
            in_specs=[pl.BlockSpec(memory_space=pl.ANY), tok(128), tok(d),
                      pl.BlockSpec((1, 1, modsel.shape[-1]), lambda b, i, dr: (2 * b + 1, 0, 0)),
                      pl.BlockSpec((1, d), lambda b, i, dr: (0, 0))],
            out_specs=pl.BlockSpec((1, tm, d), lambda b, i, dr: (b, i, 0)),
            scratch_shapes=[pltpu.VMEM((tm, d), F32), pltpu.VMEM((tm, d), F32),
                            pltpu.SemaphoreType.DMA(())]),
        out_shape=jax.ShapeDtypeStruct((bsz, n - n_ctx_tiles * tm, d), F32),
        compiler_params=_cparams(("arbitrary", "arbitrary")),
        name="moe_combine",
    )(dest, y, route, xc, modsel, g_final)


def _routing_tables(route, counts, n_tok):
    e_flat = route[:, :TOP_K].astype(jnp.int32).reshape(n_tok * TOP_K)
    rank = route[:, 4:4 + TOP_K].astype(jnp.int32).reshape(n_tok * TOP_K)
    counts = counts[0, :N_EXPERTS].astype(jnp.int32)
    padded = (counts + MOE_BLOCK - 1) // MOE_BLOCK * MOE_BLOCK
    pad_end = jnp.cumsum(padded)
    pad_start = pad_end - padded
    dest = pad_start[e_flat] + rank
    n_blocks = (n_tok * TOP_K + N_EXPERTS * (MOE_BLOCK - 1) + MOE_BLOCK - 1) // MOE_BLOCK
    block_row = jnp.arange(n_blocks, dtype=jnp.int32) * MOE_BLOCK
    block_expert = jnp.minimum(
        jnp.sum((pad_end[None, :] <= block_row[:, None]).astype(jnp.int32), axis=1),
        N_EXPERTS - 1).astype(jnp.int32)
    first = jnp.concatenate([jnp.ones((1,), jnp.int32),
                             (block_expert[1:] != block_expert[:-1]).astype(jnp.int32)])
    n_used = (pad_end[-1] // MOE_BLOCK).astype(jnp.int32).reshape(1)
    return dest.astype(jnp.int32), block_expert, first, n_used, n_blocks


def kernel(x, c, ctx, c_ctx, w_ada, b_ada, g_mix, g_ffn, g_final, w_in, dn_conv, dn_a_log,
           dn_dt_bias, dn_norm_g, hg_lb_logits, hg_norm_g, w_br_dn, w_br_hg, w_out,
           w_router_grp, b_router_grp, w_router_exp, b_router_exp, w_exp_gate, w_exp_up,
           w_exp_down):
    bsz, n_lat, d = x.shape
    n_ctx = ctx.shape[1]
    depth = w_ada.shape[0]
    n = n_ctx + n_lat
    n_tok = bsz * n
    tm = 256
    n_ctx_tiles = n_ctx // tm
    hk = HEADS * DK

    lb_w = jax.nn.softmax(hg_lb_logits.astype(F32), axis=0)
    lower_bounds = jnp.cumsum(lb_w, axis=0) - lb_w[0]

    cc = jnp.zeros((16, d), F32).at[:bsz].set(c).at[bsz].set(c_ctx)
    mod_all = _modulation(cc, w_ada, b_ada)

    xc = jnp.concatenate([ctx, x], axis=1)

    o_qkv, o_z, o_b, o_a = 0, 3 * hk, 4 * hk, 4 * hk + 2 * HEADS
    o_hq = o_a + 2 * HEADS
    o_hf, o_hi, o_hg, o_gt = o_hq + hk, o_hq + 3 * hk, o_hq + 4 * hk, o_hq + 5 * hk

    for layer in range(depth):
        final = layer == depth - 1
        mod = mod_all[layer]
        modsel = jnp.stack([jnp.broadcast_to(mod[bsz], (bsz, 6 * d)), mod[:bsz]],
                           axis=1).reshape(2 * bsz, 1, 6 * d)
        wl = w_in[layer]
        w_main = jnp.concatenate([wl[:, o_gt:], wl[:, o_qkv:o_b], wl[:, o_hq:o_gt]],
                                 axis=1).astype(BF16)
        wb = wl[:, o_b:o_a].reshape(d, 2, HEADS)
        wa = wl[:, o_a:o_hq].reshape(d, 2, HEADS)
        w_small_t = jnp.concatenate([wb, wa, jnp.zeros((d, 4, HEADS), F32)], axis=1)
        w_small_t = w_small_t.transpose(2, 1, 0).reshape(8 * HEADS, d)
        pad4 = jnp.zeros((HEADS, 4, 1), F32)
        zero2 = jnp.zeros((HEADS, 2, 1), F32)
        alog = jnp.concatenate([zero2, dn_a_log[layer].T[:, :, None], pad4], axis=1)
        dtb = jnp.concatenate([zero2, dn_dt_bias[layer].T[:, :, None], pad4], axis=1)

        if layer == 0:
            proj, small = _projection(xc, modsel, g_mix[layer][None, :], w_main, w_small_t,
                                      n_ctx_tiles, tm)
        else:
            xc, proj, small = _combine_projection(
                dest, y, route, x_mid, modsel_prev, modsel, g_mix[layer][None, :], w_main,
                w_small_t, n_ctx_tiles, tm)
        odn = _deltanet(proj, small, dn_conv[layer], alog, dtb, n_ctx)

        ohg = _hgrn2(proj, lower_bounds[layer][None, :], n_ctx)

        wr = jnp.zeros((d, 128), F32).at[:, :N_GROUPS].set(w_router_grp[layer])
        wr = wr.at[:, N_GROUPS:N_GROUPS + N_EXPERTS].set(w_router_exp[layer])
        br = jnp.zeros((1, 128), F32).at[0, :N_GROUPS].set(b_router_grp[layer])
        br = br.at[0, N_GROUPS:N_GROUPS + N_EXPERTS].set(b_router_exp[layer])
        x_mid, h2, route, counts = _merge(
            proj, odn, ohg, xc, modsel, w_br_dn[layer].astype(BF16), w_br_hg[layer].astype(BF16),
            w_out[layer].astype(BF16), dn_norm_g[layer][None, :], hg_norm_g[layer][None, :],
            g_ffn[layer][None, :], wr, br, n_ctx_tiles, tm)

        dest, block_expert, first, n_used, n_blocks = _routing_tables(
            route.reshape(n_tok, 128), counts, n_tok)
        rows = _dispatch(dest, h2.reshape(n_tok, d), n_blocks * MOE_BLOCK, tm)
        y = _experts(block_expert + layer * N_EXPERTS, first, n_used, rows,
                     w_exp_gate.reshape(depth * N_EXPERTS, d, -1),
                     w_exp_up.reshape(depth * N_EXPERTS, d, -1),
                     w_exp_down.reshape(depth * N_EXPERTS, -1, d))
        modsel_prev = modsel

    return _final_combine(dest, y, route, x_mid, modsel, g_final[None, :], n_ctx_tiles, tm)
```

```python
import functools

import jax
import jax.numpy as jnp
from jax import lax
from jax.experimental import pallas as pl
from jax.experimental.pallas import tpu as pltpu

F32 = jnp.float32
BF16 = jnp.bfloat16
HI = lax.Precision.HIGHEST

GRID_W = 64
CHUNK = 64
EPS = 1e-6
F_TINY = 1e-30
HEADS = 4
DK = 128
N_GROUPS = 4
EXPERTS_PER_GROUP = 8
N_EXPERTS = 32
TOP_K = 2
MOE_BLOCK = 512
SUB = 16
EXP_CLAMP = 60.0
DN_INTRA_UNROLL = 6
HG_STEP_UNROLL = 4
HG_INTRA_UNROLL = 8
MERGE_COLS = 256
ROW_DMA_UNROLL = 8

C_GATES = 0
C_DNQKV = 2048
C_DNZ = 3584
C_HGQ = 4096
C_HGF = 4608
C_HGI = 5632
C_HGG = 6144
N_MAIN = 6656

VMEM_LIMIT = 56 * 1024 * 1024


def _cparams(sem):
    return pltpu.CompilerParams(dimension_semantics=sem, vmem_limit_bytes=VMEM_LIMIT)


def _silu(x):
    return x * jax.nn.sigmoid(x)


def _softplus(x):
    return jnp.maximum(x, 0.0) + jnp.log1p(jnp.exp(-jnp.abs(x)))


def _dot(a, b, prec=None):
    return jnp.dot(a, b, precision=prec, preferred_element_type=F32)


def _dot_nt(a, b, prec=None):
    return lax.dot_general(a, b, (((1,), (1,)), ((), ())), precision=prec,
                           preferred_element_type=F32)


def _dot_tn(a, b, prec=None):
    return lax.dot_general(a, b, (((0,), (0,)), ((), ())), precision=prec,
                           preferred_element_type=F32)


def _bdot(a, b):
    return _dot(a.astype(BF16), b.astype(BF16))


def _bdot_nt(a, b):
    return _dot_nt(a.astype(BF16), b.astype(BF16))


def _bdot_tn(a, b):
    return _dot_tn(a.astype(BF16), b.astype(BF16))


def _mod_kernel(cc_ref, w_ref, b_ref, o_ref):
    s = _silu(cc_ref[...])
    o_ref[0] = _dot(s, w_ref[0], HI) + b_ref[0]


def _modulation(cc, w_ada, b_ada):
    depth, d, n6 = w_ada.shape
    tn = n6 // 4
    return pl.pallas_call(
        _mod_kernel,
        grid=(depth, n6 // tn),
        in_specs=[pl.BlockSpec((16, d), lambda l, j: (0, 0)),
                  pl.BlockSpec((1, d, tn), lambda l, j: (l, 0, j)),
                  pl.BlockSpec((1, 1, tn), lambda l, j: (l, 0, j))],
        out_specs=pl.BlockSpec((1, 16, tn), lambda l, j: (l, 0, j)),
        out_shape=jax.ShapeDtypeStruct((depth, 16, n6), F32),
        compiler_params=_cparams(("parallel", "parallel")),
        name="modulation",
    )(cc, w_ada, b_ada.reshape(depth, 1, n6))


def _proj_kernel(x_ref, mod_ref, g_ref, w_ref, ws_ref, p_ref, s_ref):
    d = x_ref.shape[-1]
    x = x_ref[0]
    m = mod_ref[0]
    h = x * lax.rsqrt(jnp.mean(x * x, axis=-1, keepdims=True) + EPS) * g_ref[...]
    h = h * (1.0 + m[:, d:2 * d]) + m[:, :d]
    p_ref[0] = _dot(h.astype(BF16), w_ref[...])
    s_ref[0] = _dot_nt(ws_ref[...], h, HI)


def _combine_proj_kernel(dest_ref, y_ref, rt_ref, x_ref, modp_ref, mod_ref, g_ref, w_ref, ws_ref,
                         xo_ref, p_ref, s_ref, bufs, sems, *, tile):
    d = x_ref.shape[-1]
    steps_per_batch = pl.num_programs(1)
    step = pl.program_id(0) * steps_per_batch + pl.program_id(1)
    last = pl.num_programs(0) * steps_per_batch - 1
    slot = step % 2

    def row_copy(src, sl, k, r):
        return pltpu.make_async_copy(y_ref.at[pl.ds(src, 1)], bufs.at[sl, k, pl.ds(r, 1)],
                                     sems.at[sl])

    def issue_row(base, sl, r):
        t = base + r
        row_copy(dest_ref[TOP_K * t], sl, 0, r).start()
        row_copy(dest_ref[TOP_K * t + 1], sl, 1, r).start()

    def drain(sl):
        def wait_row(r, c):
            row_copy(0, sl, 0, 0).wait()
            row_copy(0, sl, 1, 0).wait()
            return c

        lax.fori_loop(0, tile, wait_row, 0, unroll=ROW_DMA_UNROLL)

    @pl.when(step == 0)
    def _():
        def first(r, c):
            issue_row(0, 0, r)
            return c

        lax.fori_loop(0, tile, first, 0, unroll=ROW_DMA_UNROLL)

    drain(slot)
    rt = rt_ref[0]
    f = rt[:, 2:3] * bufs[slot, 0] + rt[:, 3:4] * bufs[slot, 1]
    x = x_ref[0] + modp_ref[0][:, 5 * d:6 * d] * f
    xo_ref[0] = x

    next_base = jnp.minimum(step + 1, last) * tile
    for r in range(tile):
        issue_row(next_base, 1 - slot, r)

    m = mod_ref[0]
    h = x * lax.rsqrt(jnp.mean(x * x, axis=-1, keepdims=True) + EPS) * g_ref[...]
    h = h * (1.0 + m[:, d:2 * d]) + m[:, :d]
    p_ref[0] = _dot(h.astype(BF16), w_ref[...])
    s_ref[0] = _dot_nt(ws_ref[...], h, HI)

    @pl.when(step == last)
    def _():
        drain(1 - slot)


def _combine_projection(dest, y, route, x_mid, modsel_prev, modsel, g, w_main, w_small_t,
                        n_ctx_tiles, tm):
    bsz, n, d = x_mid.shape
    n_main = w_main.shape[1]
    tok = lambda w: pl.BlockSpec((1, tm, w), lambda b, i, dr: (b, i, 0))
    mod_spec = pl.BlockSpec(
        (1, 1, modsel.shape[-1]),
        lambda b, i, dr: (2 * b + (i >= n_ctx_tiles).astype(jnp.int32), 0, 0))
    return pl.pallas_call(
        functools.partial(_combine_proj_kernel, tile=tm),
        grid_spec=pltpu.PrefetchScalarGridSpec(
            num_scalar_prefetch=1,
            grid=(bsz, n // tm),
            in_specs=[pl.BlockSpec(memory_space=pl.ANY), tok(128), tok(d), mod_spec, mod_spec,
                      pl.BlockSpec((1, d), lambda b, i, dr: (0, 0)),
                      pl.BlockSpec((d, n_main), lambda b, i, dr: (0, 0),
                                   pipeline_mode=pl.Buffered(1)),
                      pl.BlockSpec((32, d), lambda b, i, dr: (0, 0))],
            out_specs=[tok(d), tok(n_main),
                       pl.BlockSpec((1, 32, tm), lambda b, i, dr: (b, 0, i))],
            scratch_shapes=[pltpu.VMEM((2, TOP_K, tm, d), F32), pltpu.SemaphoreType.DMA((2,))]),
        out_shape=[jax.ShapeDtypeStruct((bsz, n, d), F32),
                   jax.ShapeDtypeStruct((bsz, n, n_main), F32),
                   jax.ShapeDtypeStruct((bsz, 32, n), F32)],
        compiler_params=_cparams(("arbitrary", "arbitrary")),
        name="combine_projection",
    )(dest, y, route, x_mid, modsel_prev, modsel, g, w_main, w_small_t)


def _projection(xc, modsel, g, w_main, w_small_t, n_ctx_tiles, tm):
    bsz, n, d = xc.shape
    n_main = w_main.shape[1]
    return pl.pallas_call(
        _proj_kernel,
        grid=(bsz, n // tm),
        in_specs=[pl.BlockSpec((1, tm, d), lambda b, i: (b, i, 0)),
                  pl.BlockSpec((1, 1, modsel.shape[-1]),
                               lambda b, i: (2 * b + (i >= n_ctx_tiles).astype(jnp.int32), 0, 0)),
                  pl.BlockSpec((1, d), lambda b, i: (0, 0)),
                  pl.BlockSpec((d, n_main), lambda b, i: (0, 0), pipeline_mode=pl.Buffered(1)),
                  pl.BlockSpec((32, d), lambda b, i: (0, 0))],
        out_specs=[pl.BlockSpec((1, tm, n_main), lambda b, i: (b, i, 0)),
                   pl.BlockSpec((1, 32, tm), lambda b, i: (b, 0, i))],
        out_shape=[jax.ShapeDtypeStruct((bsz, n, n_main), F32),
                   jax.ShapeDtypeStruct((bsz, 32, n), F32)],
        compiler_params=_cparams(("parallel", "parallel")),
        name="projection",
    )(xc, modsel, g, w_main, w_small_t)


def _tri_masks(fwd):
    ii = lax.broadcasted_iota(jnp.int32, (CHUNK, CHUNK), 0)
    jj = lax.broadcasted_iota(jnp.int32, (CHUNK, CHUNK), 1)
    if fwd:
        return ii >= jj, ii > jj
    return ii <= jj, ii < jj


def _unit_tri_solve(a, rhs, mm):
    ii = lax.broadcasted_iota(jnp.int32, a[0].shape, 0)
    jj = lax.broadcasted_iota(jnp.int32, a[0].shape, 1)
    same = (ii // SUB) == (jj // SUB)
    eye = jnp.where(ii == jj, 1.0, 0.0)
    each = lambda f, *xs: [f(*t) for t in zip(*xs)]
    axpy = lambda x, y: each(lambda xi, yi: xi + mm(xi, yi), x, y)
    dm = each(lambda t: jnp.where(same, t, 0.0), a)
    lm = each(lambda t: jnp.where(same, 0.0, t), a)
    d2 = each(mm, dm, dm)
    d4 = each(mm, d2, d2)
    d8 = each(mm, d4, d4)
    p = axpy(each(lambda t: eye - t, dm), d2)
    p = axpy(p, d4)
    td = axpy(p, d8)
    m = each(mm, td, lm)
    m2 = each(mm, m, m)
    t1 = each(mm, td, rhs)
    t2 = each(lambda x, y: x + mm(y, x), t1, m2)
    return each(lambda x, y: x - mm(y, x), t2, m)


def _chunk_schedule(s, n_ctx_chunks, n_chunks):
    cf = s
    cb = jnp.where(s < n_ctx_chunks, n_ctx_chunks - 1 - s, n_chunks + n_ctx_chunks - 1 - s)
    return pl.multiple_of(cf * CHUNK, CHUNK), pl.multiple_of(cb * CHUNK, CHUNK)


def _seq_cumsum(x, fwd):
    n = x.shape[0]
    pos = lax.broadcasted_iota(jnp.int32, x.shape, 0) % CHUNK
    s = 1
    while s < CHUNK:
        if fwd:
            x = x + jnp.where(pos >= s, pltpu.roll(x, s, 0), 0.0)
        else:
            x = x + jnp.where(pos < CHUNK - s, pltpu.roll(x, n - s, 0), 0.0)
        s *= 2
    return x


def _dn_intra(q, k, v, beta, gc, fwds):
    m = q[0].shape[0]
    ii = lax.broadcasted_iota(jnp.int32, (m, m), 0)
    jj = lax.broadcasted_iota(jnp.int32, (m, m), 1)
    blk = ii // CHUNK
    same = blk == (jj // CHUNK)
    is_fwd = functools.reduce(lambda x, y: x | y,
                              [blk == g for g, f in enumerate(fwds) if f], blk < 0)
    incl = same & ((is_fwd & (ii >= jj)) | (~is_fwd & (ii <= jj)))
    strict = incl & (ii != jj)
    wide = lambda t: jnp.concatenate([t] * (m // DK), axis=1)
    tall = lambda t: jnp.concatenate([t] * (m // DK), axis=0)
    each = lambda f, *xs: [f(*t) for t in zip(*xs)]
    kk = each(_bdot_nt, k, k)
    qk = each(_bdot_nt, q, k)
    dec = each(lambda g: jnp.where(incl, jnp.exp(jnp.where(incl, wide(g) - tall(g.T), 0.0)), 0.0), gc)
    a = each(lambda x, b, d: jnp.where(strict, x * wide(b) * d, 0.0), kk, beta, dec)
    eg = each(jnp.exp, gc)
    rhs = each(lambda vi, ki, b, e: jnp.concatenate([vi * b, ki * (b * e)], axis=1), v, k, beta, eg)
    sol = _unit_tri_solve(a, rhs, _bdot)

    def to_end(g):
        rows = [g[i * CHUNK + (CHUNK - 1 if f else 0)][None, :] for i, f in enumerate(fwds)]
        return jnp.concatenate([jnp.broadcast_to(r, (CHUNK, DK)) for r in rows], axis=0) - g

    n_stack = len(fwds)
    ri = lax.broadcasted_iota(jnp.int32, (n_stack * DK, m), 0) // DK
    ci = lax.broadcasted_iota(jnp.int32, (n_stack * DK, m), 1) // CHUNK
    kdt_bd = each(lambda ki, g: jnp.where(
        ri == ci, jnp.concatenate([(ki * jnp.exp(to_end(g))).T] * n_stack, axis=0), 0.0), k, gc)
    att = each(lambda x, d: jnp.where(incl, x * d, 0.0), qk, dec)
    att_uw = each(_bdot, att, sol)
    kd_uw = each(_bdot, kdt_bd, sol)
    q_eff = each(lambda x, e, aw: x * e - aw[:, DK:], q, eg, att_uw)
    decay = each(lambda g: [jnp.exp(g[i * CHUNK + (CHUNK - 1 if f else 0)][None, :])
                            for i, f in enumerate(fwds)], gc)
    return [(aw[:, :DK], qe, kw, dc) for aw, qe, kw, dc in zip(att_uw, q_eff, kd_uw, decay)]


DN_PAIR_ORDER = ((0, 2), (3, 1))


def _dn_compose(stacks):
    blk = lambda t, g, size: t[g * size:(g + 1) * size]
    lhs, rhs = [], []
    for o0, qe, kw, dc in stacks:
        for first, second in DN_PAIR_ORDER:
            lhs.append(jnp.concatenate([blk(kw, second, DK)[:, DK:], blk(qe, second, CHUNK)], axis=0))
            rhs.append(blk(kw, first, DK))
    z = [_bdot(a, b) for a, b in zip(lhs, rhs)]
    out = []
    for s, (o0, qe, kw, dc) in enumerate(stacks):
        o0b = [blk(o0, g, CHUNK) for g in range(4)]
        qb = [blk(qe, g, CHUNK) for g in range(4)]
        pairs = []
        for di, (first, second) in enumerate(DN_PAIR_ORDER):
            zi = z[2 * s + di]
            k1, k2 = blk(kw, first, DK), blk(kw, second, DK)
            e1, e2 = dc[first], dc[second]
            c12 = e2 * k1[:, :DK] + k2[:, :DK] - zi[:DK, :DK]
            n12 = e2 * k1[:, DK:] + e1 * k2[:, DK:] - zi[:DK, DK:]
            o0b[second] = o0b[second] + zi[DK:, :DK]
            qb[second] = e1 * qb[second] - zi[DK:, DK:]
            pairs.append((c12, n12, e1 * e2))
        out.append((o0b, qb, pairs))
    return out


def _dn_kernel(q_ref, k_ref, v_ref, cq_ref, ck_ref, cv_ref, sm_ref, alog_ref, dtb_ref,
               o_ref, qs, ks, vs, bfs, bbs, gfs, gbs, qes, cs, ns, es, *, n_ctx):
    n = q_ref.shape[1]
    row = lax.broadcasted_iota(jnp.int32, (n, DK), 0)
    seg_first = (row == 0) | (row == n_ctx)
    seg_last = (row == n_ctx - 1) | (row == n - 1)

    def conv_act(x_ref, c_ref):
        x = x_ref[0]
        w = c_ref[...]
        xm = jnp.where(seg_first, 0.0, pltpu.roll(x, 1, 0))
        xp = jnp.where(seg_last, 0.0, pltpu.roll(x, n - 1, 0))
        return _silu(xm * w[0:1] + x * w[1:2] + xp * w[2:3])

    def l2n(y):
        return y * lax.rsqrt(jnp.sum(y * y, axis=-1, keepdims=True) + EPS)

    qs[...] = l2n(conv_act(q_ref, cq_ref)) * (DK ** -0.5)
    ks[...] = l2n(conv_act(k_ref, ck_ref))
    vs[...] = conv_act(v_ref, cv_ref)

    sm = sm_ref[0]
    beta = jax.nn.sigmoid(sm)
    g = -jnp.exp(alog_ref[0]) * _softplus(sm + dtb_ref[0])
    pos = lax.broadcasted_iota(jnp.int32, sm.shape, 1) % CHUNK
    pre, suf = g, g
    s = 1
    while s < CHUNK:
        pre = pre + jnp.where(pos >= s, pltpu.roll(pre, s, 1), 0.0)
        suf = suf + jnp.where(pos < CHUNK - s, pltpu.roll(suf, n - s, 1), 0.0)
        s *= 2

    def col(r):
        return jnp.broadcast_to(r, (DK, n)).T

    bfs[...] = col(beta[0:1])
    bbs[...] = col(beta[1:2])
    gfs[...] = col(pre[2:3])
    gbs[...] = col(suf[3:4])

    n_chunks = n // CHUNK
    n_ctx_chunks = n_ctx // CHUNK

    def intra(c, carry):
        starts = [pl.multiple_of((c * DN_INTRA_UNROLL + j) * 2 * CHUNK, 2 * CHUNK)
                  for j in range(DN_INTRA_UNROLL)]
        pairs = [pl.ds(r0, 2 * CHUNK) for r0 in starts]
        stack = lambda t: jnp.concatenate([t[:CHUNK], t[:CHUNK], t[CHUNK:], t[CHUNK:]], axis=0)
        both = lambda tf, tb: jnp.concatenate(
            [tf[:CHUNK], tb[:CHUNK], tf[CHUNK:], tb[CHUNK:]], axis=0)
        results = _dn_intra(
            [stack(qs[p, :]) for p in pairs], [stack(ks[p, :]) for p in pairs],
            [stack(vs[p, :]) for p in pairs],
            [both(bfs[p, :], bbs[p, :]) for p in pairs],
            [both(gfs[p, :], gbs[p, :]) for p in pairs], (True, False, True, False))
        for r0, (o0, q_eff, pair_maps) in zip(starts, _dn_compose(results)):
            pair = pl.ds(r0, 2 * CHUNK)
            o_ref[0, pair, :] = jnp.concatenate([o0[0] + o0[1], o0[2] + o0[3]], axis=0)
            for di, (c12, n12, e12) in enumerate(pair_maps):
                qes[di, pair, :] = jnp.concatenate([q_eff[di], q_eff[2 + di]], axis=0).astype(BF16)
                cs[di, pair, :] = c12
                ns[di, pair, :] = n12.astype(BF16)
                es[di, pl.ds(r0 // (2 * CHUNK), 1), :] = e12
        return carry

    lax.fori_loop(0, n_chunks // (2 * DN_INTRA_UNROLL), intra, 0)

    def step(s, states):
        first_b = jnp.where(2 * s < n_ctx_chunks, n_ctx_chunks - 1 - 2 * s,
                            n_chunks + n_ctx_chunks - 1 - 2 * s)
        pair_idx = (s, first_b // 2)
        pairs = [pl.ds(pl.multiple_of(p * 2 * CHUNK, 2 * CHUNK), 2 * CHUNK) for p in pair_idx]
        stb = [st.astype(BF16) for st in states]
        corr = [_dot(ns[di, pairs[di], :], stb[di]) for di in range(2)]
        outs = [_dot(qes[di, pairs[di], :], stb[di]) for di in range(2)]
        new_states = []
        for di in range(2):
            decay = es[di, pl.ds(pair_idx[di], 1), :]
            new_states.append(states[di] * decay + cs[di, pairs[di], :] - corr[di])
            o_ref[0, pairs[di], :] = o_ref[0, pairs[di], :] + outs[di]
        return tuple(new_states)

    zero = jnp.zeros((DK, DK), F32)
    lax.fori_loop(0, n_chunks // 2, step, (zero, zero))


def _deltanet(proj, small, conv_w, alog, dtb, n_ctx):
    bsz, n, _ = proj.shape
    qb = C_DNQKV // DK
    seq = lambda off: pl.BlockSpec((1, n, DK), lambda b, h: (b, 0, off + h))
    cw = lambda off: pl.BlockSpec((3, DK), lambda b, h: (0, off + h))
    return pl.pallas_call(
        functools.partial(_dn_kernel, n_ctx=n_ctx),
        grid=(bsz, HEADS),
        in_specs=[seq(qb), seq(qb + HEADS), seq(qb + 2 * HEADS),
                  cw(0), cw(HEADS), cw(2 * HEADS),
                  pl.BlockSpec((1, 8, n), lambda b, h: (b, h, 0)),
                  pl.BlockSpec((1, 8, 1), lambda b, h: (h, 0, 0)),
                  pl.BlockSpec((1, 8, 1), lambda b, h: (h, 0, 0))],
        out_specs=pl.BlockSpec((1, n, DK), lambda b, h: (b, 0, h)),
        out_shape=jax.ShapeDtypeStruct((bsz, n, HEADS * DK), F32),
        scratch_shapes=[pltpu.VMEM((n, DK), F32) for _ in range(7)] + [
            pltpu.VMEM((2, n, DK), BF16),
            pltpu.VMEM((2, n, DK), F32),
            pltpu.VMEM((2, n, DK), BF16),
            pltpu.VMEM((2, -(-n // (16 * CHUNK)) * 8, DK), F32)],
        compiler_params=_cparams(("parallel", "parallel")),
        name="deltanet",
    )(proj, proj, proj, conv_w, conv_w, conv_w, small, alog, dtb)


HG_FWDS = (True, False, True, False)


def _hg_intra(q, zf, zb, v, lb, causal_ref, blockdiag_ref):
    fwds = HG_FWDS
    n_stack = len(fwds)
    nsub = CHUNK // SUB

    def gate(z):
        f = lb + (1.0 - lb) * jax.nn.sigmoid(z)
        return (1.0 - lb) * jax.nn.sigmoid(-z), jnp.log(jnp.maximum(f, F_TINY))

    kf_f, lf_f = gate(zf)
    kf_b, lf_b = gate(zb)
    both = lambda tf, tb: jnp.concatenate([tf[:CHUNK], tb[:CHUNK], tf[CHUNK:], tb[CHUNK:]], axis=0)
    stack = lambda t: both(t, t)
    q_st = stack(_silu(q))
    v_st = stack(v)
    kf = both(kf_f, kf_b)
    gc = both(_seq_cumsum(lf_f, True), _seq_cumsum(lf_b, False))

    def ref_row(g, i):
        r = g * CHUNK + i * SUB + (0 if fwds[g] else SUB - 1)
        return gc[r:r + 1, :]

    rep = lambda r, k: jnp.broadcast_to(r, (k, DK))
    qe = q_st * jnp.exp(gc - jnp.concatenate(
        [rep(ref_row(g, i), SUB) for g in range(n_stack) for i in range(nsub)], axis=0))
    scores = []
    for i in range(nsub):
        ref = jnp.concatenate([rep(ref_row(g, i), CHUNK) for g in range(n_stack)], axis=0)
        ke = kf * jnp.exp(jnp.minimum(ref - gc, EXP_CLAMP))
        lhs = jnp.concatenate([qe[g * CHUNK + i * SUB:g * CHUNK + (i + 1) * SUB]
                               for g in range(n_stack)], axis=0)
        scores.append(_bdot_nt(lhs, ke))
    att = jnp.concatenate([scores[i][g * SUB:(g + 1) * SUB]
                           for g in range(n_stack) for i in range(nsub)], axis=0)
    o0 = _bdot(att * causal_ref[...], v_st)

    end_rows = [gc[g * CHUNK + (CHUNK - 1 if f else 0)][None, :] for g, f in enumerate(fwds)]
    kd = kf * jnp.exp(jnp.concatenate([rep(r, CHUNK) for r in end_rows], axis=0) - gc)
    vt_bd = jnp.concatenate([v_st.T] * n_stack, axis=0) * blockdiag_ref[...]
    ct = _bdot(vt_bd, kd)
    return o0, q_st * jnp.exp(gc), ct, [jnp.exp(r) for r in end_rows]


def _hg_masks(causal_ref, blockdiag_ref):
    m = len(HG_FWDS) * CHUNK
    ii = lax.broadcasted_iota(jnp.int32, (m, m), 0)
    jj = lax.broadcasted_iota(jnp.int32, (m, m), 1)
    blk = ii // CHUNK
    is_fwd = functools.reduce(lambda x, y: x | y,
                              [blk == g for g, f in enumerate(HG_FWDS) if f], blk < 0)
    incl = (blk == (jj // CHUNK)) & ((is_fwd & (ii >= jj)) | (~is_fwd & (ii <= jj)))
    causal_ref[...] = jnp.where(incl, 1.0, 0.0)
    ri = lax.broadcasted_iota(jnp.int32, blockdiag_ref.shape, 0) // DK
    ci = lax.broadcasted_iota(jnp.int32, blockdiag_ref.shape, 1) // CHUNK
    blockdiag_ref[...] = jnp.where(ri == ci, 1.0, 0.0)


def _hg_kernel(q_ref, zf_ref, zb_ref, v_ref, lb_ref, o_ref, oscan, qgs, cts, es, causal, blockdiag,
               *, n_ctx):
    n = q_ref.shape[1]
    n_lat = n - n_ctx
    col_len = n_lat // GRID_W
    cols_per_pair = 2 * CHUNK // col_len
    lb = lb_ref[...]
    _hg_masks(causal, blockdiag)

    def store(r0, res):
        o0, qg, ct, decay = res
        for g in range(4):
            di = g % 2
            rg = r0 + (g // 2) * CHUNK
            sl = pl.ds(rg, CHUNK)
            if di == 0:
                oscan[sl, :] = o0[g * CHUNK:(g + 1) * CHUNK] + o0[(g + 1) * CHUNK:(g + 2) * CHUNK]
            qgs[di, sl, :] = qg[g * CHUNK:(g + 1) * CHUNK].astype(BF16)
            cts[di, pl.ds(pl.multiple_of(2 * rg, 2 * CHUNK), DK), :] = ct[g * DK:(g + 1) * DK]
            es[di, pl.ds(rg // CHUNK, 1), :] = decay[g]

    def ctx_pair(i, c):
        r0 = pl.multiple_of(i * 2 * CHUNK, 2 * CHUNK)
        sl = pl.ds(r0, 2 * CHUNK)
        store(r0, _hg_intra(q_ref[0, sl, :], zf_ref[0, sl, :], zb_ref[0, sl, :], v_ref[0, sl, :], lb,
                            causal, blockdiag))
        return c

    lax.fori_loop(0, n_ctx // (2 * CHUNK), ctx_pair, 0, unroll=HG_INTRA_UNROLL)

    def lat_pair(i, c):
        def load(ref):
            return jnp.concatenate(
                [ref[0, pl.ds(n_ctx + cols_per_pair * i + j, col_len, stride=GRID_W), :]
                 for j in range(cols_per_pair)], axis=0)

        r0 = pl.multiple_of(n_ctx + i * 2 * CHUNK, 2 * CHUNK)
        store(r0, _hg_intra(load(q_ref), load(zf_ref), load(zb_ref), load(v_ref), lb,
                            causal, blockdiag))
        return c

    lax.fori_loop(0, n_lat // (2 * CHUNK), lat_pair, 0, unroll=HG_INTRA_UNROLL)

    n_chunks = n // CHUNK
    n_ctx_chunks = n_ctx // CHUNK

    def steps(t, states):
        for j in range(HG_STEP_UNROLL):
            rows = _chunk_schedule(t * HG_STEP_UNROLL + j, n_ctx_chunks, n_chunks)
            sls = [pl.ds(r0, CHUNK) for r0 in rows]
            outs = [_dot_nt(qgs[di, sls[di], :], states[di].astype(BF16)) for di in range(2)]
            new_states = []
            for di in range(2):
                decay = es[di, pl.ds(rows[di] // CHUNK, 1), :]
                c = cts[di, pl.ds(pl.multiple_of(2 * rows[di], 2 * CHUNK), DK), :]
                new_states.append(states[di] * decay + c)
                oscan[sls[di], :] = oscan[sls[di], :] + outs[di]
            states = tuple(new_states)
        return states

    zero = jnp.zeros((DK, DK), F32)
    lax.fori_loop(0, n_chunks // HG_STEP_UNROLL, steps, (zero, zero))

    o_ref[0, pl.ds(0, n_ctx), :] = oscan[pl.ds(0, n_ctx), :]

    def grid_row(r, c):
        dst = pl.ds(pl.multiple_of(n_ctx + r * GRID_W, GRID_W), GRID_W)
        o_ref[0, dst, :] = oscan[pl.ds(n_ctx + r, GRID_W, stride=col_len), :]
        return c

    lax.fori_loop(0, col_len, grid_row, 0)


def _hgrn2(proj, lb, n_ctx):
    bsz, n, _ = proj.shape
    seq = lambda off: pl.BlockSpec((1, n, DK), lambda b, h: (b, 0, off // DK + h))
    n_chunks = n // CHUNK
    return pl.pallas_call(
        functools.partial(_hg_kernel, n_ctx=n_ctx),
        grid=(bsz, HEADS),
        in_specs=[seq(C_HGQ), seq(C_HGF), seq(C_HGF + HEADS * DK), seq(C_HGI),
                  pl.BlockSpec((1, DK), lambda b, h: (0, h))],
        out_specs=pl.BlockSpec((1, n, DK), lambda b, h: (b, 0, h)),
        out_shape=jax.ShapeDtypeStruct((bsz, n, HEADS * DK), F32),
        scratch_shapes=[pltpu.VMEM((n, DK), F32),
                        pltpu.VMEM((2, n, DK), BF16),
                        pltpu.VMEM((2, 2 * n, DK), F32),
                        pltpu.VMEM((2, -(-n_chunks // 8) * 8, DK), F32),
                        pltpu.VMEM((len(HG_FWDS) * CHUNK, len(HG_FWDS) * CHUNK), F32),
                        pltpu.VMEM((len(HG_FWDS) * DK, len(HG_FWDS) * CHUNK), F32)],
        compiler_params=_cparams(("parallel", "parallel")),
        name="hgrn2",
    )(proj, proj, proj, proj, lb)


def _gated_rmsnorm(o, z, g):
    parts = []
    for h in range(HEADS):
        oh = o[:, h * DK:(h + 1) * DK]
        zh = z[:, h * DK:(h + 1) * DK]
        y = oh * lax.rsqrt(jnp.mean(oh * oh, axis=-1, keepdims=True) + EPS) * g
        parts.append(y * _silu(zh))
    return jnp.concatenate(parts, axis=1)


def _route(lg):
    lane = lax.broadcasted_iota(jnp.int32, lg.shape, 1).astype(F32)
    neg = -1e30
    big = 1e9
    is_grp = lane < N_GROUPS
    gl = jnp.where(is_grp, lg, neg)
    gmax = jnp.max(gl, axis=-1, keepdims=True)
    gsel = jnp.min(jnp.where(gl == gmax, lane, big), axis=-1, keepdims=True)
    gp = 1.0 / jnp.sum(jnp.where(is_grp, jnp.exp(gl - gmax), 0.0), axis=-1, keepdims=True)
    lo = N_GROUPS + EXPERTS_PER_GROUP * gsel
    el = jnp.where((lane >= lo) & (lane < lo + EXPERTS_PER_GROUP), lg, neg)
    m1 = jnp.max(el, axis=-1, keepdims=True)
    i1 = jnp.min(jnp.where(el == m1, lane, big), axis=-1, keepdims=True)
    el2 = jnp.where(lane == i1, neg, el)
    m2 = jnp.max(el2, axis=-1, keepdims=True)
    i2 = jnp.min(jnp.where(el2 == m2, lane, big), axis=-1, keepdims=True)
    r = jnp.exp(m2 - m1)
    g1 = gp / (1.0 + r)
    g2 = g1 * r
    out = jnp.where(lane == 0, i1 - N_GROUPS, 0.0)
    out = jnp.where(lane == 1, i2 - N_GROUPS, out)
    out = jnp.where(lane == 2, g1, out)
    return jnp.where(lane == 3, g2, out)


def _merge_kernel(gate_ref, z_ref, hg_ref, odn_ref, ohg_ref, x_ref, mod_ref, wdn_ref, whg_ref,
                  wout_ref, gdn_ref, ghg_ref, gffn_ref, wrh_ref, wrl_ref, br_ref,
                  xo_ref, h2_ref, rt_ref, cnt_ref, mix_s):
    d = x_ref.shape[-1]
    cols = [slice(j * MERGE_COLS, (j + 1) * MERGE_COLS) for j in range(d // MERGE_COLS)]
    mod = lambda k, cs: mod_ref[0, :, k * d + cs.start:k * d + cs.stop]

    @pl.when((pl.program_id(0) == 0) & (pl.program_id(1) == 0))
    def _():
        cnt_ref[...] = jnp.zeros(cnt_ref.shape, F32)

    a_dn = _gated_rmsnorm(odn_ref[0], z_ref[0], gdn_ref[...]).astype(BF16)
    a_hg = _gated_rmsnorm(ohg_ref[0], hg_ref[0], ghg_ref[...]).astype(BF16)
    for cs in cols:
        br_dn = _dot(a_dn, wdn_ref[:, cs])
        br_hg = _dot(a_hg, whg_ref[:, cs])
        g_dn = jax.nn.sigmoid(gate_ref[0, :, cs])
        g_hg = jax.nn.sigmoid(gate_ref[0, :, d + cs.start:d + cs.stop])
        mix_s[:, cs] = (g_dn * br_dn + g_hg * br_hg).astype(BF16)
    mix = mix_s[...]
    ssq = jnp.zeros((x_ref.shape[1], 1), F32)
    for cs in cols:
        x = x_ref[0, :, cs] + mod(2, cs) * _dot(mix, wout_ref[:, cs])
        xo_ref[0, :, cs] = x
        ssq = ssq + jnp.sum(x * x, axis=-1, keepdims=True)
    scale = lax.rsqrt(ssq * (1.0 / d) + EPS)
    lg = jnp.zeros(rt_ref.shape[1:], F32) + br_ref[...]
    for cs in cols:
        h2 = xo_ref[0, :, cs] * scale * gffn_ref[:, cs] * (1.0 + mod(4, cs)) + mod(3, cs)
        h2_ref[0, :, cs] = h2
        h_hi = h2.astype(BF16)
        h_lo = (h2 - h_hi.astype(F32)).astype(BF16)
        lg = lg + (_dot(h_hi, wrh_ref[cs, :]) + (_dot(h_hi, wrl_ref[cs, :]) + _dot(h_lo, wrh_ref[cs, :])))
    rt = _route(lg)
    tm = rt.shape[0]
    lane = lax.broadcasted_iota(jnp.int32, rt.shape, 1)
    lanef = lane.astype(F32)
    oh1 = jnp.where(lanef == rt[:, 0:1], 1.0, 0.0)
    oh2 = jnp.where(lanef == rt[:, 1:2], 1.0, 0.0)
    both = oh1 + oh2
    ii = lax.broadcasted_iota(jnp.int32, (tm, tm), 0)
    jj = lax.broadcasted_iota(jnp.int32, (tm, tm), 1)
    earlier = jnp.where(ii > jj, 1.0, 0.0).astype(BF16)
    prior = _dot(earlier, both.astype(BF16)) + cnt_ref[0:1, :]
    rank1 = jnp.sum(prior * oh1, axis=-1, keepdims=True)
    rank2 = jnp.sum(prior * oh2, axis=-1, keepdims=True)
    rt = jnp.where(lane == 4, rank1, rt)
    rt_ref[0] = jnp.where(lane == 5, rank2, rt)
    cnt_ref[0:1, :] = cnt_ref[0:1, :] + jnp.sum(both, axis=0, keepdims=True)


def _merge(proj, odn, ohg, xc, modsel, wdn, whg, wout, gdn, ghg, gffn, wr, br, n_ctx_tiles, tm):
    wr_hi = wr.astype(BF16)
    wr_lo = (wr - wr_hi.astype(F32)).astype(BF16)
    bsz, n, d = xc.shape
    hv = HEADS * DK
    tok = lambda w, j: pl.BlockSpec((1, tm, w), lambda b, i: (b, i, j))
    full = lambda a: pl.BlockSpec(a.shape, lambda b, i: (0,) * a.ndim)
    return pl.pallas_call(
        _merge_kernel,
        grid=(bsz, n // tm),
        in_specs=[tok(2 * d, C_GATES // (2 * d)), tok(hv, C_DNZ // hv), tok(hv, C_HGG // hv),
                  tok(hv, 0), tok(hv, 0), tok(d, 0),
                  pl.BlockSpec((1, 1, modsel.shape[-1]),
                               lambda b, i: (2 * b + (i >= n_ctx_tiles).astype(jnp.int32), 0, 0)),
                  full(wdn), full(whg), full(wout), full(gdn), full(ghg), full(gffn),
                  full(wr_hi), full(wr_lo), full(br)],
        out_specs=[tok(d, 0), tok(d, 0), tok(128, 0),
                   pl.BlockSpec((8, 128), lambda b, i: (0, 0))],
        out_shape=[jax.ShapeDtypeStruct((bsz, n, d), F32),
                   jax.ShapeDtypeStruct((bsz, n, d), F32),
                   jax.ShapeDtypeStruct((bsz, n, 128), F32),
                   jax.ShapeDtypeStruct((8, 128), F32)],
        scratch_shapes=[pltpu.VMEM((tm, d), BF16)],
        compiler_params=_cparams(("arbitrary", "arbitrary")),
        name="merge",
    )(proj, proj, proj, odn, ohg, xc, modsel, wdn, whg, wout, gdn, ghg, gffn, wr_hi, wr_lo, br)


def _dispatch_kernel(dest_ref, h_ref, init_ref, rows_ref, sem, *, tile):
    del init_ref
    base = pl.program_id(0) * tile

    def row_copy(r, d):
        return pltpu.make_async_copy(h_ref.at[pl.ds(r, 1)], rows_ref.at[pl.ds(d, 1)], sem)

    def issue(r, c):
        t = base + r
        row_copy(r, dest_ref[TOP_K * t]).start()
        row_copy(r, dest_ref[TOP_K * t + 1]).start()
        return c

    lax.fori_loop(0, tile, issue, 0, unroll=ROW_DMA_UNROLL)

    def drain(r, c):
        row_copy(0, 0).wait()
        row_copy(0, 0).wait()
        return c

    lax.fori_loop(0, tile, drain, 0, unroll=ROW_DMA_UNROLL)


def _dispatch(dest, h2, n_rows, tile):
    n_tok, d = h2.shape
    return pl.pallas_call(
        functools.partial(_dispatch_kernel, tile=tile),
        grid_spec=pltpu.PrefetchScalarGridSpec(
            num_scalar_prefetch=1,
            grid=(n_tok // tile,),
            in_specs=[pl.BlockSpec((tile, d), lambda i, dr: (i, 0)),
                      pl.BlockSpec(memory_space=pl.ANY)],
            out_specs=pl.BlockSpec(memory_space=pl.ANY),
            scratch_shapes=[pltpu.SemaphoreType.DMA(())]),
        out_shape=jax.ShapeDtypeStruct((n_rows, d), F32),
        input_output_aliases={2: 0},
        compiler_params=pltpu.CompilerParams(dimension_semantics=("arbitrary",),
                                             has_side_effects=True),
        name="moe_dispatch",
    )(dest, h2, jnp.zeros((n_rows, d), F32))


def _expert_kernel(be_ref, first_ref, nused_ref, x_ref, wg_ref, wu_ref, wd_ref, o_ref,
                   wg_s, wu_s, wd_s):
    i = pl.program_id(0)

    @pl.when(first_ref[i] == 1)
    def _():
        wg_s[...] = wg_ref[0].astype(BF16)
        wu_s[...] = wu_ref[0].astype(BF16)
        wd_s[...] = wd_ref[0].astype(BF16)

    @pl.when(i < nused_ref[0])
    def _():
        x = x_ref[...].astype(BF16)
        a = _silu(_dot(x, wg_s[...])) * _dot(x, wu_s[...])
        o_ref[...] = _dot(a.astype(BF16), wd_s[...])

    @pl.when(i >= nused_ref[0])
    def _():
        o_ref[...] = jnp.zeros(o_ref.shape, F32)


def _experts(block_expert, first, n_used, rows, wg, wu, wd):
    n_rows, d = rows.shape
    de = wg.shape[-1]
    return pl.pallas_call(
        _expert_kernel,
        grid_spec=pltpu.PrefetchScalarGridSpec(
            num_scalar_prefetch=3,
            grid=(n_rows // MOE_BLOCK,),
            in_specs=[pl.BlockSpec((MOE_BLOCK, d), lambda i, be, f, nu: (jnp.minimum(i, nu[0] - 1), 0)),
                      pl.BlockSpec((1, d, de), lambda i, be, f, nu: (be[i], 0, 0)),
                      pl.BlockSpec((1, d, de), lambda i, be, f, nu: (be[i], 0, 0)),
                      pl.BlockSpec((1, de, d), lambda i, be, f, nu: (be[i], 0, 0))],
            out_specs=pl.BlockSpec((MOE_BLOCK, d), lambda i, be, f, nu: (i, 0)),
            scratch_shapes=[pltpu.VMEM((d, de), BF16), pltpu.VMEM((d, de), BF16),
                            pltpu.VMEM((de, d), BF16)]),
        out_shape=jax.ShapeDtypeStruct((n_rows, d), F32),
        compiler_params=_cparams(("arbitrary",)),
        name="moe_experts",
    )(block_expert, first, n_used, rows, wg, wu, wd)


def _combine_kernel(dest_ref, y_ref, rt_ref, x_ref, mod_ref, gfin_ref, o_ref, buf0, buf1, sem,
                    *, tile, tiles_per_batch, first_tile):
    d = x_ref.shape[-1]
    base = (pl.program_id(0) * tiles_per_batch + first_tile + pl.program_id(1)) * tile

    def row_copy(src, buf, r):
        return pltpu.make_async_copy(y_ref.at[pl.ds(src, 1)], buf.at[pl.ds(r, 1)], sem)

    def issue(r, c):
        t = base + r
        row_copy(dest_ref[TOP_K * t], buf0, r).start()
        row_copy(dest_ref[TOP_K * t + 1], buf1, r).start()
        return c

    lax.fori_loop(0, tile, issue, 0, unroll=ROW_DMA_UNROLL)

    def drain(r, c):
        row_copy(0, buf0, 0).wait()
        row_copy(0, buf1, 0).wait()
        return c

    lax.fori_loop(0, tile, drain, 0, unroll=ROW_DMA_UNROLL)

    rt = rt_ref[0]
    f = rt[:, 2:3] * buf0[...] + rt[:, 3:4] * buf1[...]
    x = x_ref[0] + mod_ref[0][:, 5 * d:6 * d] * f
    o_ref[0] = x * lax.rsqrt(jnp.mean(x * x, axis=-1, keepdims=True) + EPS) * gfin_ref[...]


def _final_combine(dest, y, route, xc, modsel, g_final, n_ctx_tiles, tm):
    bsz, n, d = xc.shape
    n_tiles = n // tm
    tok = lambda w: pl.BlockSpec((1, tm, w), lambda b, i, dr: (b, i + n_ctx_tiles, 0))
    return pl.pallas_call(
        functools.partial(_combine_kernel, tile=tm, tiles_per_batch=n_tiles,
                          first_tile=n_ctx_tiles),
        grid_spec=pltpu.PrefetchScalarGridSpec(
            num_scalar_prefetch=1,
            grid=(bsz, n_tiles - n_ctx_tiles),
            in_specs=[pl.BlockSpec(memory_space=pl.ANY), tok(128), tok(d),
                      pl.BlockSpec((1, 1, modsel.shape[-1]), lambda b, i, dr: (2 * b + 1, 0, 0)),
                      pl.BlockSpec((1, d), lambda b, i, dr: (0, 0))],
            out_specs=pl.BlockSpec((1, tm, d), lambda b, i, dr: (b, i, 0)),
            scratch_shapes=[pltpu.VMEM((tm, d), F32), pltpu.VMEM((tm, d), F32),
                            pltpu.SemaphoreType.DMA(())]),
        out_shape=jax.ShapeDtypeStruct((bsz, n - n_ctx_tiles * tm, d), F32),
        compiler_params=_cparams(("arbitrary", "arbitrary")),
        name="moe_combine",
    )(dest, y, route, xc, modsel, g_final)


def _routing_tables(route, counts, n_tok):
    e_flat = route[:, :TOP_K].astype(jnp.int32).reshape(n_tok * TOP_K)
    rank = route[:, 4:4 + TOP_K].astype(jnp.int32).reshape(n_tok * TOP_K)
    counts = counts[0, :N_EXPERTS].astype(jnp.int32)
    padded = (counts + MOE_BLOCK - 1) // MOE_BLOCK * MOE_BLOCK
    pad_end = jnp.cumsum(padded)
    pad_start = pad_end - padded
    dest = pad_start[e_flat] + rank
    n_blocks = (n_tok * TOP_K + N_EXPERTS * (MOE_BLOCK - 1) + MOE_BLOCK - 1) // MOE_BLOCK
    block_row = jnp.arange(n_blocks, dtype=jnp.int32) * MOE_BLOCK
    block_expert = jnp.minimum(
        jnp.sum((pad_end[None, :] <= block_row[:, None]).astype(jnp.int32), axis=1),
        N_EXPERTS - 1).astype(jnp.int32)
    first = jnp.concatenate([jnp.ones((1,), jnp.int32),
                             (block_expert[1:] != block_expert[:-1]).astype(jnp.int32)])
    n_used = (pad_end[-1] // MOE_BLOCK).astype(jnp.int32).reshape(1)
    return dest.astype(jnp.int32), block_expert, first, n_used, n_blocks


def kernel(x, c, ctx, c_ctx, w_ada, b_ada, g_mix, g_ffn, g_final, w_in, dn_conv, dn_a_log,
           dn_dt_bias, dn_norm_g, hg_lb_logits, hg_norm_g, w_br_dn, w_br_hg, w_out,
           w_router_grp, b_router_grp, w_router_exp, b_router_exp, w_exp_gate, w_exp_up,
           w_exp_down):
    bsz, n_lat, d = x.shape
    n_ctx = ctx.shape[1]
    depth = w_ada.shape[0]
    n = n_ctx + n_lat
    n_tok = bsz * n
    tm = 256
    n_ctx_tiles = n_ctx // tm
    hk = HEADS * DK

    lb_w = jax.nn.softmax(hg_lb_logits.astype(F32), axis=0)
    lower_bounds = jnp.cumsum(lb_w, axis=0) - lb_w[0]

    cc = jnp.zeros((16, d), F32).at[:bsz].set(c).at[bsz].set(c_ctx)
    mod_all = _modulation(cc, w_ada, b_ada)

    xc = jnp.concatenate([ctx, x], axis=1)

    o_qkv, o_z, o_b, o_a = 0, 3 * hk, 4 * hk, 4 * hk + 2 * HEADS
    o_hq = o_a + 2 * HEADS
    o_hf, o_hi, o_hg, o_gt = o_hq + hk, o_hq + 3 * hk, o_hq + 4 * hk, o_hq + 5 * hk

    for layer in range(depth):
        final = layer == depth - 1
        mod = mod_all[layer]
        modsel = jnp.stack([jnp.broadcast_to(mod[bsz], (bsz, 6 * d)), mod[:bsz]],
                           axis=1).reshape(2 * bsz, 1, 6 * d)
        wl = w_in[layer]
        w_main = jnp.concatenate([wl[:, o_gt:], wl[:, o_qkv:o_b], wl[:, o_hq:o_gt]],
                                 axis=1).astype(BF16)
        wb = wl[:, o_b:o_a].reshape(d, 2, HEADS)
        wa = wl[:, o_a:o_hq].reshape(d, 2, HEADS)
        w_small_t = jnp.concatenate([wb, wa, jnp.zeros((d, 4, HEADS), F32)], axis=1)
        w_small_t = w_small_t.transpose(2, 1, 0).reshape(8 * HEADS, d)
        pad4 = jnp.zeros((HEADS, 4, 1), F32)
        zero2 = jnp.zeros((HEADS, 2, 1), F32)
        alog = jnp.concatenate([zero2, dn_a_log[layer].T[:, :, None], pad4], axis=1)
        dtb = jnp.concatenate([zero2, dn_dt_bias[layer].T[:, :, None], pad4], axis=1)

        if layer == 0:
            proj, small = _projection(xc, modsel, g_mix[layer][None, :], w_main, w_small_t,
                                      n_ctx_tiles, tm)
        else:
            xc, proj, small = _combine_projection(
                dest, y, route, x_mid, modsel_prev, modsel, g_mix[layer][None, :], w_main,
                w_small_t, n_ctx_tiles, tm)
        odn = _deltanet(proj, small, dn_conv[layer], alog, dtb, n_ctx)

        ohg = _hgrn2(proj, lower_bounds[layer][None, :], n_ctx)

        wr = jnp.zeros((d, 128), F32).at[:, :N_GROUPS].set(w_router_grp[layer])
        wr = wr.at[:, N_GROUPS:N_GROUPS + N_EXPERTS].set(w_router_exp[layer])
        br = jnp.zeros((1, 128), F32).at[0, :N_GROUPS].set(b_router_grp[layer])
        br = br.at[0, N_GROUPS:N_GROUPS + N_EXPERTS].set(b_router_exp[layer])
        x_mid, h2, route, counts = _merge(
            proj, odn, ohg, xc, modsel, w_br_dn[layer].astype(BF16), w_br_hg[layer].astype(BF16),
            w_out[layer].astype(BF16), dn_norm_g[layer][None, :], hg_norm_g[layer][None, :],
            g_ffn[layer][None, :], wr, br, n_ctx_tiles, tm)

        dest, block_expert, first, n_used, n_blocks = _routing_tables(
            route.reshape(n_tok, 128), counts, n_tok)
        rows = _dispatch(dest, h2.reshape(n_tok, d), n_blocks * MOE_BLOCK, tm)
        y = _experts(block_expert + layer * N_EXPERTS, first, n_used, rows,
                     w_exp_gate.reshape(depth * N_EXPERTS, d, -1),
                     w_exp_up.reshape(depth * N_EXPERTS, d, -1),
                     w_exp_down.reshape(depth * N_EXPERTS, -1, d))
        modsel_prev = modsel

    return _final_combine(dest, y, route, x_mid, modsel, g_final[None, :], n_ctx_tiles, tm)
```
